```python
import jax, jax.numpy as jnp
from jax import lax
import numpy as np


D_MODEL = 1024
BATCH = 16
SEQ = 4096
DEPTH = 1

CTX_LEN = 256
GRID_W = 64
N_RET_HEADS = 4
RET_QK_DIM = 256
RET_V_DIM = 512
RET_CHUNK = 128
N_MLA_HEADS = 8
MLA_Q_RANK = 384
MLA_KV_RANK = 256
MLA_QK_NOPE = 128
MLA_QK_ROPE = 64
MLA_V_DIM = 128
Q_BLOCK = 128
N_EXPERTS = 32
TOP_K = 4
D_FF_EXPERT = D_MODEL
SWIGLU_LIMIT = 7.0
SWIGLU_ALPHA = 1.702
MOE_BLOCK = 128
ROPE_BASE = 10000.0
EPS = 1e-6
IN_COLS = (2 * N_RET_HEADS * RET_QK_DIM + 2 * N_RET_HEADS * RET_V_DIM
           + MLA_Q_RANK + MLA_KV_RANK + MLA_QK_ROPE + 2 * D_MODEL)

kernel_name = 'hybrid_retention_mla_moe_dit_block'


def rms_norm(t, g):
    tf = t.astype(jnp.float32)
    n = tf * lax.rsqrt(jnp.mean(tf * tf, axis=-1, keepdims=True) + EPS)
    return n.astype(t.dtype) * g


def head_rms(t):
    tf = t.astype(jnp.float32)
    return (tf * lax.rsqrt(jnp.mean(tf * tf, axis=-1, keepdims=True) + EPS)).astype(t.dtype)


def ada_mod(cvec, w, b):
    m = jax.nn.silu(cvec) @ w + b
    return jnp.split(m[:, None, :], 6, axis=-1)


def modulate(t, shift, scale):
    return t * (1.0 + scale) + shift


def rope_1d(t, pos):
    half = t.shape[-1] // 2
    freqs = ROPE_BASE ** (-jnp.arange(half, dtype=jnp.float32) / half)
    ang = pos.astype(jnp.float32)[:, None] * freqs[None, :]
    cos = jnp.cos(ang)[None, :, None, :].astype(t.dtype)
    sin = jnp.sin(ang)[None, :, None, :].astype(t.dtype)
    t1, t2 = t[..., :half], t[..., half:]
    return jnp.concatenate([t1 * cos - t2 * sin, t1 * sin + t2 * cos], axis=-1)


def rope_2d(t, rows, cols):
    half = t.shape[-1] // 2
    return jnp.concatenate([rope_1d(t[..., :half], rows), rope_1d(t[..., half:], cols)], axis=-1)


def split_proj(z):
    sizes = (N_RET_HEADS * RET_QK_DIM, N_RET_HEADS * RET_QK_DIM, N_RET_HEADS * RET_V_DIM,
             N_RET_HEADS * RET_V_DIM, MLA_Q_RANK, MLA_KV_RANK, MLA_QK_ROPE, D_MODEL, D_MODEL)
    idx, acc = [], 0
    for s in sizes[:-1]:
        acc += s
        idx.append(acc)
    return jnp.split(z, idx, axis=-1)


def retention_qkv(rq, rk, rv, rope_fn):
    B, L, _ = rq.shape
    q = rq.reshape(B, L, N_RET_HEADS, RET_QK_DIM)
    k = rk.reshape(B, L, N_RET_HEADS, RET_QK_DIM)
    v = rv.reshape(B, L, N_RET_HEADS, RET_V_DIM)
    if rope_fn is not None:
        q, k = rope_fn(q), rope_fn(k)
    return q, k * (RET_QK_DIM ** -0.5), v


def retention_state(k, v, lg):
    L = k.shape[1]
    w = jnp.exp(lg[:, None] * (L - 1 - jnp.arange(L, dtype=jnp.float32))[None, :]).astype(v.dtype)
    return jnp.einsum('blhd,blhe,hl->bhde', k, v, w)


def retention_chunked(q, k, v, lg, init_state, include_diag):
    B, L, H, dk = q.shape
    dv = v.shape[-1]
    nc = L // RET_CHUNK
    idx = jnp.arange(RET_CHUNK, dtype=jnp.float32)
    diff = idx[:, None] - idx[None, :]
    mask = (diff >= 0) if include_diag else (diff > 0)
    intra = jnp.where(mask[None], jnp.exp(lg[:, None, None] * jnp.maximum(diff, 0.0)[None]), 0.0).astype(v.dtype)
    q_dec = jnp.exp(lg[:, None] * (idx + 1.0)[None, :]).astype(v.dtype)
    k_dec = jnp.exp(lg[:, None] * (RET_CHUNK - 1.0 - idx)[None, :]).astype(v.dtype)
    chunk_dec = jnp.exp(lg * RET_CHUNK).astype(v.dtype)

    def to_chunks(t):
        return t.reshape(B, nc, RET_CHUNK, H, t.shape[-1]).transpose(1, 0, 3, 2, 4)

    def step(state, blk):
        qc, kc, vc = blk
        s = jnp.einsum('bhid,bhjd->bhij', qc, kc) * intra[None]
        o = (jnp.einsum('bhij,bhje->bhie', s, vc)
             + jnp.einsum('bhid,bhde->bhie', qc * q_dec[None, :, :, None], state))
        state = (state * chunk_dec[None, :, None, None]
                 + jnp.einsum('bhjd,bhje->bhde', kc * k_dec[None, :, :, None], vc))
        return state, o

    _, o = lax.scan(step, init_state, (to_chunks(q), to_chunks(k), to_chunks(v)))
    return o.transpose(1, 0, 3, 2, 4).reshape(B, L, H, dv)


def bidir_retention(q, k, v, lg_f, lg_b, s_f, s_b):
    fwd = retention_chunked(q, k, v, lg_f, s_f, True)
    bwd = retention_chunked(jnp.flip(q, 1), jnp.flip(k, 1), jnp.flip(v, 1), lg_b, s_b, False)
    return fwd + jnp.flip(bwd, 1)


def mla_qkv(cq, ckv, kr, q_norm_g, w_uq, kv_norm_g, w_ukv, rope_fn):
    B, L, _ = cq.shape
    q = (rms_norm(cq, q_norm_g) @ w_uq).reshape(B, L, N_MLA_HEADS, MLA_QK_NOPE + MLA_QK_ROPE)
    q_nope, q_rope = q[..., :MLA_QK_NOPE], q[..., MLA_QK_NOPE:]
    kv = (rms_norm(ckv, kv_norm_g) @ w_ukv).reshape(B, L, N_MLA_HEADS, MLA_QK_NOPE + MLA_V_DIM)
    k_nope, v = kv[..., :MLA_QK_NOPE], kv[..., MLA_QK_NOPE:]
    k_rope = kr[:, :, None, :]
    if rope_fn is not None:
        q_rope, k_rope = rope_fn(q_rope), rope_fn(k_rope)
    return q_nope, q_rope, k_nope, k_rope[:, :, 0, :], v


def mla_attention(q_nope, q_rope, k_nope, k_rope, v):
    B, Lq, H, _ = q_nope.shape
    nb = Lq // Q_BLOCK
    scale = (MLA_QK_NOPE + MLA_QK_ROPE) ** -0.5

    def blocks(t):
        return t.reshape(B, nb, Q_BLOCK, H, t.shape[-1]).swapaxes(0, 1)

    def one_block(qb):
        qn, qr = qb
        s = (jnp.einsum('bqhd,bkhd->bhqk', qn, k_nope)
             + jnp.einsum('bqhd,bkd->bhqk', qr, k_rope)) * scale
        p = jax.nn.softmax(s.astype(jnp.float32), axis=-1).astype(v.dtype)
        return jnp.einsum('bhqk,bkhd->bqhd', p, v)

    o = lax.map(one_block, (blocks(q_nope), blocks(q_rope)))
    return o.swapaxes(0, 1).reshape(B, Lq, H, MLA_V_DIM)


def merge_branches(ret_o, ret_gate, att_o, g_ret, g_mla, w_branch_ret, w_branch_mla, w_out):
    B, L = ret_o.shape[:2]
    r = head_rms(ret_o).reshape(B, L, N_RET_HEADS * RET_V_DIM) * jax.nn.silu(ret_gate)
    m = att_o.reshape(B, L, N_MLA_HEADS * MLA_V_DIM)
    merged = jax.nn.sigmoid(g_ret) * (r @ w_branch_ret) + jax.nn.sigmoid(g_mla) * (m @ w_branch_mla)
    return merged @ w_out


def token_mixers(h_lat, h_ctx, rope_fn, need_ctx, w_in, ret_decay_fwd, ret_decay_bwd, mla_q_norm_g,
                 mla_w_uq, mla_kv_norm_g, mla_w_ukv, w_branch_ret, w_branch_mla, w_out):
    zl = split_proj(h_lat @ w_in)
    zc = split_proj(h_ctx @ w_in)
    lg_f = jax.nn.log_sigmoid(ret_decay_fwd.astype(jnp.float32))
    lg_b = jax.nn.log_sigmoid(ret_decay_bwd.astype(jnp.float32))
    ql, kl, vl = retention_qkv(zl[0], zl[1], zl[2], rope_fn)
    qc, kc, vc = retention_qkv(zc[0], zc[1], zc[2], None)
    s_f = retention_state(kc, vc, lg_f)
    s_b = retention_state(jnp.flip(kc, 1), jnp.flip(vc, 1), lg_b)
    ret_lat = bidir_retention(ql, kl, vl, lg_f, lg_b, s_f, s_b)
    ml = mla_qkv(zl[4], zl[5], zl[6], mla_q_norm_g, mla_w_uq, mla_kv_norm_g, mla_w_ukv, rope_fn)
    mc = mla_qkv(zc[4], zc[5], zc[6], mla_q_norm_g, mla_w_uq, mla_kv_norm_g, mla_w_ukv, None)
    att_lat = mla_attention(ml[0], ml[1], jnp.concatenate([mc[2], ml[2]], axis=1),
                            jnp.concatenate([mc[3], ml[3]], axis=1), jnp.concatenate([mc[4], ml[4]], axis=1))
    y_lat = merge_branches(ret_lat, zl[3], att_lat, zl[7], zl[8], w_branch_ret, w_branch_mla, w_out)
    y_ctx = None
    if need_ctx:
        zeros = jnp.zeros_like(s_f)
        ret_ctx = bidir_retention(qc, kc, vc, lg_f, lg_b, zeros, zeros)
        att_ctx = mla_attention(mc[0], mc[1], mc[2], mc[3], mc[4])
        y_ctx = merge_branches(ret_ctx, zc[3], att_ctx, zc[7], zc[8], w_branch_ret, w_branch_mla, w_out)
    return y_lat, y_ctx


def moe(h, router_w, router_b, w_gu, b_gu, w_down, b_down):
    B, L, D = h.shape
    tok = h.reshape(-1, D)
    n_tok = tok.shape[0]
    logits = tok @ router_w + router_b
    top_val, top_idx = lax.top_k(logits, TOP_K)
    gates = jax.nn.softmax(top_val.astype(jnp.float32), axis=-1)
    nk = n_tok * TOP_K
    flat_e = top_idx.reshape(-1)
    flat_tok = jnp.arange(nk, dtype=jnp.int32) // TOP_K
    order = jnp.argsort(flat_e)
    sorted_e = flat_e[order]
    sorted_tok = flat_tok[order]
    sorted_gate = gates.reshape(-1)[order]
    counts = jnp.zeros((N_EXPERTS,), jnp.int32).at[flat_e].add(1)
    padded = (counts + MOE_BLOCK - 1) // MOE_BLOCK * MOE_BLOCK
    start = jnp.cumsum(counts) - counts
    pad_end = jnp.cumsum(padded)
    pad_start = pad_end - padded
    dest = pad_start[sorted_e] + (jnp.arange(nk, dtype=jnp.int32) - start[sorted_e])
    n_rows = nk + N_EXPERTS * MOE_BLOCK
    n_blocks = n_rows // MOE_BLOCK
    xs = jnp.zeros((n_rows, D), tok.dtype).at[dest].set(tok[sorted_tok])
    block_e = jnp.minimum(jnp.searchsorted(pad_end, jnp.arange(n_blocks, dtype=jnp.int32) * MOE_BLOCK,
                                           side='right'), N_EXPERTS - 1)

    def expert_block(args):
        xb, e = args
        gu = xb @ w_gu[e] + b_gu[e]
        gate, up = gu[:, :D_FF_EXPERT], gu[:, D_FF_EXPERT:]
        gate = jnp.minimum(gate, SWIGLU_LIMIT)
        up = jnp.clip(up, -SWIGLU_LIMIT, SWIGLU_LIMIT)
        glu = gate * jax.nn.sigmoid(SWIGLU_ALPHA * gate)
        return ((up + 1.0) * glu) @ w_down[e] + b_down[e]

    ys = lax.map(expert_block, (xs.reshape(n_blocks, MOE_BLOCK, D), block_e)).reshape(n_rows, D)
    out = jnp.zeros_like(tok).at[sorted_tok].add(ys[dest] * sorted_gate[:, None].astype(ys.dtype))
    return out.reshape(B, L, D)


def setup_inputs(seed: int = 0) -> dict:
    key = jax.random.key(seed)
    ks = jax.random.split(key, 25)
    f32 = jnp.float32

    def nrm(k, shape, scale):
        return jax.random.normal(k, shape, f32) * scale

    ret_base = jnp.log(jnp.exp2(5.0 + jnp.arange(N_RET_HEADS, dtype=f32)) - 1.0)
    ret_v_w = N_RET_HEADS * RET_V_DIM
    mla_v_w = N_MLA_HEADS * MLA_V_DIM
    return {
        'x': nrm(ks[0], (BATCH, SEQ, D_MODEL), 1.0),
        'c': nrm(ks[1], (BATCH, D_MODEL), 1.0),
        'ctx': nrm(ks[2], (BATCH, CTX_LEN, D_MODEL), 1.0),
        'c_ctx': nrm(ks[3], (D_MODEL,), 1.0),
        'norm1_g': 1.0 + nrm(ks[4], (DEPTH, D_MODEL), 0.02),
        'norm2_g': 1.0 + nrm(ks[5], (DEPTH, D_MODEL), 0.02),
        'ada_w': nrm(ks[6], (DEPTH, D_MODEL, 6 * D_MODEL), 0.5 * D_MODEL ** -0.5),
        'ada_b': nrm(ks[7], (DEPTH, 6 * D_MODEL), 0.01),
        'w_in': nrm(ks[8], (DEPTH, D_MODEL, IN_COLS), D_MODEL ** -0.5),
        'ret_decay_fwd': ret_base[None, :] + nrm(ks[9], (DEPTH, N_RET_HEADS), 0.1),
        'ret_decay_bwd': ret_base[None, :] + nrm(ks[10], (DEPTH, N_RET_HEADS), 0.1),
        'mla_q_norm_g': 1.0 + nrm(ks[11], (DEPTH, MLA_Q_RANK), 0.02),
        'mla_w_uq': nrm(ks[12], (DEPTH, MLA_Q_RANK, N_MLA_HEADS * (MLA_QK_NOPE + MLA_QK_ROPE)), MLA_Q_RANK ** -0.5),
        'mla_kv_norm_g': 1.0 + nrm(ks[13], (DEPTH, MLA_KV_RANK), 0.02),
        'mla_w_ukv': nrm(ks[14], (DEPTH, MLA_KV_RANK, N_MLA_HEADS * (MLA_QK_NOPE + MLA_V_DIM)), MLA_KV_RANK ** -0.5),
        'w_branch_ret': nrm(ks[15], (DEPTH, ret_v_w, D_MODEL), ret_v_w ** -0.5),
        'w_branch_mla': nrm(ks[16], (DEPTH, mla_v_w, D_MODEL), mla_v_w ** -0.5),
        'w_out': nrm(ks[17], (DEPTH, D_MODEL, D_MODEL), D_MODEL ** -0.5),
        'router_w': nrm(ks[18], (DEPTH, D_MODEL, N_EXPERTS), D_MODEL ** -0.5),
        'router_b': nrm(ks[19], (DEPTH, N_EXPERTS), 0.01),
        'exp_w_gu': nrm(ks[20], (DEPTH, N_EXPERTS, D_MODEL, 2 * D_FF_EXPERT), D_MODEL ** -0.5),
        'exp_b_gu': nrm(ks[21], (DEPTH, N_EXPERTS, 2 * D_FF_EXPERT), 0.01),
        'exp_w_down': nrm(ks[22], (DEPTH, N_EXPERTS, D_FF_EXPERT, D_MODEL), D_FF_EXPERT ** -0.5),
        'exp_b_down': nrm(ks[23], (DEPTH, N_EXPERTS, D_MODEL), 0.01),
        'final_norm_g': 1.0 + nrm(ks[24], (D_MODEL,), 0.02),
    }


def reference(x, c, ctx, c_ctx, norm1_g, norm2_g, ada_w, ada_b, w_in, ret_decay_fwd, ret_decay_bwd,
              mla_q_norm_g, mla_w_uq, mla_kv_norm_g, mla_w_ukv, w_branch_ret, w_branch_mla, w_out,
              router_w, router_b, exp_w_gu, exp_b_gu, exp_w_down, exp_b_down, final_norm_g):
    L = x.shape[1]
    n_grid_rows = L // GRID_W
    rows = jnp.repeat(jnp.arange(n_grid_rows, dtype=jnp.int32), GRID_W)
    cols = jnp.tile(jnp.arange(GRID_W, dtype=jnp.int32), n_grid_rows)
    rope_fn = lambda t: rope_2d(t, rows, cols)
    for layer in range(DEPTH):
        need_ctx = layer < DEPTH - 1
        m_lat = ada_mod(c, ada_w[layer], ada_b[layer])
        m_ctx = ada_mod(c_ctx[None, :], ada_w[layer], ada_b[layer])
        h_lat = modulate(rms_norm(x, norm1_g[layer]), m_lat[0], m_lat[1])
        h_ctx = modulate(rms_norm(ctx, norm1_g[layer]), m_ctx[0], m_ctx[1])
        y_lat, y_ctx = token_mixers(h_lat, h_ctx, rope_fn, need_ctx, w_in[layer], ret_decay_fwd[layer],
                                    ret_decay_bwd[layer], mla_q_norm_g[layer], mla_w_uq[layer],
                                    mla_kv_norm_g[layer], mla_w_ukv[layer], w_branch_ret[layer],
                                    w_branch_mla[layer], w_out[layer])
        x = x + m_lat[2] * y_lat
        h2 = modulate(rms_norm(x, norm2_g[layer]), m_lat[3], m_lat[4])
        x = x + m_lat[5] * moe(h2, router_w[layer], router_b[layer], exp_w_gu[layer], exp_b_gu[layer],
                               exp_w_down[layer], exp_b_down[layer])
        if need_ctx:
            ctx = ctx + m_ctx[2] * y_ctx
            h2c = modulate(rms_norm(ctx, norm2_g[layer]), m_ctx[3], m_ctx[4])
            ctx = ctx + m_ctx[5] * moe(h2c, router_w[layer], router_b[layer], exp_w_gu[layer],
                                       exp_b_gu[layer], exp_w_down[layer], exp_b_down[layer])
    return rms_norm(x, final_norm_g)
```

```python
import functools

import numpy as np
import jax
import jax.numpy as jnp
from jax import lax
from jax.experimental import pallas as pl
from jax.experimental.pallas import tpu as pltpu

GRID_W = 64
N_RET_HEADS = 4
RET_QK_DIM = 256
RET_V_DIM = 512
N_MLA_HEADS = 8
MLA_Q_RANK = 384
MLA_KV_RANK = 256
MLA_QK_NOPE = 128
MLA_QK_ROPE = 64
MLA_V_DIM = 128
N_EXPERTS = 32
TOP_K = 4
SWIGLU_LIMIT = 7.0
SWIGLU_ALPHA = 1.702
ROPE_BASE = 10000.0
EPS = 1e-6

LANES = 128
MLA_HEAD_PAD = 2 * LANES
RET_CHUNK = 256
VMEM_LIMIT = 56 * 1024 * 1024

F32 = jnp.float32
BF16 = jnp.bfloat16


def _cparams(sem):
    return pltpu.CompilerParams(dimension_semantics=sem, vmem_limit_bytes=VMEM_LIMIT)


def _resident(shape):
    nd = len(shape)
    return pl.BlockSpec(shape, lambda *_: (0,) * nd, pipeline_mode=pl.Buffered(1))


def _dot(a, b):
    return jnp.dot(a, b, preferred_element_type=F32)


def _dot_nt(a, b):
    return lax.dot_general(a, b, (((1,), (1,)), ((), ())), preferred_element_type=F32)


def _dot_tn(a, b):
    return lax.dot_general(a, b, (((0,), (0,)), ((), ())), preferred_element_type=F32)


def _rms(x):
    return x * lax.rsqrt(jnp.mean(x * x, axis=-1, keepdims=True) + EPS)


def _sigmoid(x):
    return 1.0 / (1.0 + jnp.exp(-x))


def _ada_kernel(c_ref, w_ref, b_ref, o_ref):
    c = c_ref[...]
    s = c * _sigmoid(c)
    o_ref[...] = jnp.dot(s, w_ref[...], preferred_element_type=F32,
                         precision=lax.Precision.HIGHEST) + b_ref[...]


def _ada_mod(cvec, w, b):
    rows, d = cvec.shape
    n = w.shape[1]
    tn = 1024
    return pl.pallas_call(
        _ada_kernel,
        out_shape=jax.ShapeDtypeStruct((rows, n), F32),
        grid=(n // tn,),
        in_specs=[pl.BlockSpec((rows, d), lambda j: (0, 0)),
                  pl.BlockSpec((d, tn), lambda j: (0, j)),
                  pl.BlockSpec((1, tn), lambda j: (0, j))],
        out_specs=pl.BlockSpec((rows, tn), lambda j: (0, j)),
        compiler_params=_cparams(("arbitrary",)),
        name="ada_mod",
    )(cvec, w, b.reshape(1, n))


def _rope_tables(seq_len, qk_scale):
    pos = np.arange(seq_len)
    rows = (pos // GRID_W).astype(np.float32)
    cols = (pos % GRID_W).astype(np.float32)

    def angles(p, half):
        freqs = (np.float32(ROPE_BASE) ** (-np.arange(half, dtype=np.float32) / np.float32(half))).astype(np.float32)
        return (p[:, None] * freqs[None, :]).astype(np.float32)

    a_r, a_c = angles(rows, 64), angles(cols, 64)
    ret = np.concatenate([np.cos(a_r), np.cos(a_r), np.cos(a_c), np.cos(a_c),
                          -np.sin(a_r), np.sin(a_r), -np.sin(a_c), np.sin(a_c)], axis=1)
    b_r, b_c = angles(rows, 16), angles(cols, 16)
    z32 = np.zeros((seq_len, 32), np.float32)
    z64 = np.zeros((seq_len, 64), np.float32)
    cos = np.concatenate([np.cos(b_r), np.cos(b_c), np.cos(b_r), np.cos(b_c), z64], axis=1)
    sin_up = np.concatenate([-np.sin(b_r), -np.sin(b_c), z32, z64], axis=1)
    sin_dn = np.concatenate([z32, np.sin(b_r), np.sin(b_c), z64], axis=1)
    mla_k = np.concatenate([cos, sin_up, sin_dn], axis=1)
    mla = np.concatenate([mla_k, mla_k * np.float32(qk_scale)], axis=1)
    return jnp.asarray(ret, F32), jnp.asarray(mla, F32)


def _rope_ret(t, tab, parity):
    cos = tab[:, parity * LANES:(parity + 1) * LANES]
    sin = tab[:, (2 + parity) * LANES:(3 + parity) * LANES]
    return t * cos + pltpu.roll(t, 64, 1) * sin


def _rope_mla(t, tab, base):
    cos = tab[:, base:base + LANES]
    sin_up = tab[:, base + LANES:base + 2 * LANES]
    sin_dn = tab[:, base + 2 * LANES:base + 3 * LANES]
    return t * cos + pltpu.roll(t, 96, 1) * sin_up + pltpu.roll(t, 32, 1) * sin_dn


_C_Q, _C_K, _C_V, _C_G = 0, 1024, 2048, 4096
_C_CQ, _C_CKV, _C_KR, _C_GR, _C_GM, _C_END = 6144, 6528, 6784, 6912, 7936, 8960
_X_K, _X_V, _X_CKV, _X_KR, _X_END = 0, 1024, 3072, 3328, 3456


def _norm_mod(x, g, shift, scale):
    return (_rms(x) * g) * (1.0 + scale) + shift


def _mla_kv(ckv_acc, kr, gkv_ref, wukv_ref, k_ref, v_ref):
    ckvn = (_rms(ckv_acc) * gkv_ref[...]).astype(BF16)
    kn = _dot(ckvn, wukv_ref[:, 0:N_MLA_HEADS * MLA_QK_NOPE])
    krb = kr.astype(BF16)
    for hh in range(N_MLA_HEADS):
        k_ref[0, :, hh * MLA_HEAD_PAD:hh * MLA_HEAD_PAD + LANES] = kn[:, hh * LANES:(hh + 1) * LANES].astype(BF16)
        k_ref[0, :, hh * MLA_HEAD_PAD + LANES:(hh + 1) * MLA_HEAD_PAD] = krb
    vv = _dot(ckvn, wukv_ref[:, N_MLA_HEADS * MLA_QK_NOPE:])
    v_ref[0] = vv.astype(BF16)


def _inproj_lat_kernel(qk_scale, x_ref, shift_ref, scale_ref, g_ref, w_ref, wuq_ref, wukv_ref, gq_ref, gkv_ref,
                       tret_ref, tmla_ref,
                       rq_ref, rk_ref, rv_ref, sgate_ref, q_ref, k_ref, v_ref, sgr_ref, sgm_ref):
    h = _norm_mod(x_ref[0], g_ref[...], shift_ref[0], scale_ref[0]).astype(BF16)
    tret = tret_ref[...]
    tmla = tmla_ref[...]

    def proj(c0, n):
        return _dot(h, w_ref[:, c0:c0 + n])

    for base, out in ((_C_Q, rq_ref), (_C_K, rk_ref)):
        for j in range(2):
            acc = proj(base + j * 512, 512)
            for blk in range(4):
                t = acc[:, blk * LANES:(blk + 1) * LANES]
                col = j * 512 + blk * LANES
                out[0, :, col:col + LANES] = _rope_ret(t, tret, blk % 2).astype(BF16)
    for j in range(4):
        rv_ref[0, :, j * 512:(j + 1) * 512] = proj(_C_V + j * 512, 512).astype(BF16)
    for j in range(4):
        a = proj(_C_G + j * 512, 512)
        sgate_ref[0, :, j * 512:(j + 1) * 512] = (a * _sigmoid(a)).astype(BF16)
    for j in range(2):
        a = proj(_C_GR + j * 512, 512)
        sgr_ref[0, :, j * 512:(j + 1) * 512] = _sigmoid(a).astype(BF16)
    for j in range(2):
        a = proj(_C_GM + j * 512, 512)
        sgm_ref[0, :, j * 512:(j + 1) * 512] = _sigmoid(a).astype(BF16)

    cqn = (_rms(proj(_C_CQ, MLA_Q_RANK)) * gq_ref[...]).astype(BF16)
    for j in range(4):
        acc = _dot(cqn, wuq_ref[:, j * 512:(j + 1) * 512])
        for blk in range(4):
            t = acc[:, blk * LANES:(blk + 1) * LANES]
            col = j * 512 + blk * LANES
            if blk % 2 == 0:
                q_ref[0, :, col:col + LANES] = (t * qk_scale).astype(BF16)
            else:
                q_ref[0, :, col:col + LANES] = _rope_mla(t, tmla, 3 * LANES).astype(BF16)
    kr = _rope_mla(proj(_C_KR, LANES), tmla, 0)
    _mla_kv(proj(_C_CKV, MLA_KV_RANK), kr, gkv_ref, wukv_ref, k_ref, v_ref)


def _inproj_ctx_kernel(x_ref, shift_ref, scale_ref, g_ref, w_ref, wukv_ref, gkv_ref,
                       rk_ref, rv_ref, k_ref, v_ref):
    h = _norm_mod(x_ref[0], g_ref[...], shift_ref[...], scale_ref[...]).astype(BF16)

    def proj(c0, n):
        return _dot(h, w_ref[:, c0:c0 + n])

    for j in range(2):
        rk_ref[0, :, j * 512:(j + 1) * 512] = proj(_X_K + j * 512, 512).astype(BF16)
    for j in range(4):
        rv_ref[0, :, j * 512:(j + 1) * 512] = proj(_X_V + j * 512, 512).astype(BF16)
    _mla_kv(proj(_X_CKV, MLA_KV_RANK), proj(_X_KR, LANES), gkv_ref, wukv_ref, k_ref, v_ref)


def _prep_weights(w_in, w_uq, w_ukv):
    nq = N_RET_HEADS * RET_QK_DIM
    nv = N_RET_HEADS * RET_V_DIM
    d = w_in.shape[0]
    sizes = (nq, nq, nv, nv, MLA_Q_RANK, MLA_KV_RANK, MLA_QK_ROPE, d, d)
    offs = np.cumsum((0,) + sizes)
    wq, wk, wv, wg, wcq, wckv, wkr, wgr, wgm = [w_in[:, offs[i]:offs[i + 1]] for i in range(9)]
    wk = wk * (RET_QK_DIM ** -0.5)
    perm = np.concatenate([np.arange(0, 16), np.arange(32, 48), np.arange(16, 32), np.arange(48, 64)])
    wkr = jnp.pad(wkr[:, perm], ((0, 0), (0, LANES - MLA_QK_ROPE)))
    w_lat = jnp.concatenate([wq, wk, wv, wg, wcq, wckv, wkr, wgr, wgm], axis=1).astype(BF16)
    w_ctx = jnp.concatenate([wk, wv, wckv, wkr], axis=1).astype(BF16)
    uq = w_uq.reshape(MLA_Q_RANK, N_MLA_HEADS, MLA_QK_NOPE + MLA_QK_ROPE)
    uq = jnp.concatenate([uq[:, :, :MLA_QK_NOPE], uq[:, :, MLA_QK_NOPE:][:, :, perm],
                          jnp.zeros((MLA_Q_RANK, N_MLA_HEADS, LANES - MLA_QK_ROPE), w_uq.dtype)], axis=2)
    uq = uq.reshape(MLA_Q_RANK, N_MLA_HEADS * MLA_HEAD_PAD).astype(BF16)
    ukv = w_ukv.reshape(MLA_KV_RANK, N_MLA_HEADS, MLA_QK_NOPE + MLA_V_DIM)
    ukv = jnp.concatenate([ukv[:, :, :MLA_QK_NOPE].reshape(MLA_KV_RANK, -1),
                           ukv[:, :, MLA_QK_NOPE:].reshape(MLA_KV_RANK, -1)], axis=1).astype(BF16)
    return w_lat, w_ctx, uq, ukv


def _inproj_lat(x, shift, scale, g, w_lat, uq, ukv, gq, gkv, tret, tmla, qk_scale):
    b, l, d = x.shape
    tm = 256
    tok = lambda n: pl.BlockSpec((1, tm, n), lambda i, j: (i, j, 0))
    vec = pl.BlockSpec((1, 1, d), lambda i, j: (i, 0, 0))
    out_w = (1024, 1024, 2048, 2048, 2048, 2048, 1024, 1024, 1024)
    return pl.pallas_call(
        functools.partial(_inproj_lat_kernel, qk_scale),
        out_shape=[jax.ShapeDtypeStruct((b, l, n), BF16) for n in out_w],
        grid=(b, l // tm),
        in_specs=[tok(d), vec, vec, _resident((1, d)), _resident(w_lat.shape), _resident(uq.shape),
                  _resident(ukv.shape), _resident((1, MLA_Q_RANK)), _resident((1, MLA_KV_RANK)),
                  pl.BlockSpec((tm, tret.shape[1]), lambda i, j: (j, 0)),
                  pl.BlockSpec((tm, tmla.shape[1]), lambda i, j: (j, 0))],
        out_specs=[tok(n) for n in out_w],
        compiler_params=_cparams(("arbitrary", "arbitrary")),
        name="inproj_lat",
    )(x, shift, scale, g, w_lat, uq, ukv, gq, gkv, tret, tmla)


def _inproj_ctx(ctx, shift, scale, g, w_ctx, ukv, gkv):
    b, l, d = ctx.shape
    tm = 256
    tok = lambda n: pl.BlockSpec((1, tm, n), lambda i, j: (i, j, 0))
    out_w = (1024, 2048, 2048, 1024)
    return pl.pallas_call(
        _inproj_ctx_kernel,
        out_shape=[jax.ShapeDtypeStruct((b, l, n), BF16) for n in out_w],
        grid=(b, l // tm),
        in_specs=[tok(d), _resident((1, d)), _resident((1, d)), _resident((1, d)), _resident(w_ctx.shape),
                  _resident(ukv.shape), _resident((1, MLA_KV_RANK))],
        out_specs=[tok(n) for n in out_w],
        compiler_params=_cparams(("arbitrary", "arbitrary")),
        name="inproj_ctx",
    )(ctx, shift, scale, g, w_ctx, ukv, gkv)


def _retention_kernel(n_chunks, n_ctx_chunks, lg_ref, q_ref, k_ref, v_ref, kc_ref, vc_ref, o_ref,
                      ob_ref, sf_ref, sb_ref, dmat_ref, dec_ref):
    c = RET_CHUNK
    head = pl.program_id(1)
    lgf = lg_ref[0, head]
    lgb = lg_ref[1, head]
    ri = lax.broadcasted_iota(jnp.int32, (c, c), 0).astype(F32)
    ci = lax.broadcasted_iota(jnp.int32, (c, c), 1).astype(F32)
    diff = ri - ci
    dmat_ref[...] = jnp.where(diff >= 0, jnp.exp(lgf * jnp.maximum(diff, 0.0)),
                              jnp.exp(lgb * jnp.maximum(-diff, 0.0)))
    rk = lax.broadcasted_iota(jnp.int32, (c, RET_QK_DIM), 0).astype(F32)
    dec_ref[0] = jnp.exp(lgf * (rk + 1.0))
    dec_ref[1] = jnp.exp(lgf * (c - 1.0 - rk))
    dec_ref[2] = jnp.exp(lgb * (c - rk))
    dec_ref[3] = jnp.exp(lgb * rk)
    cdf = jnp.exp(jnp.full((1, RET_V_DIM), lgf * c, F32))
    cdb = jnp.exp(jnp.full((1, RET_V_DIM), lgb * c, F32))

    def scaled(t, which):
        return (t.astype(F32) * dec_ref[which]).astype(BF16)

    sf_ref[...] = jnp.zeros_like(sf_ref)
    sb_ref[...] = jnp.zeros_like(sb_ref)
    for n in range(n_ctx_chunks):
        sl = slice(n * c, (n + 1) * c)
        sf_ref[...] = sf_ref[...] * cdf + _dot_tn(scaled(kc_ref[0, sl, :], 1), vc_ref[0, sl, :])
    for n in reversed(range(n_ctx_chunks)):
        sl = slice(n * c, (n + 1) * c)
        sb_ref[...] = sb_ref[...] * cdb + _dot_tn(scaled(kc_ref[0, sl, :], 3), vc_ref[0, sl, :])

    def bwd_step(i, carry):
        r0 = pl.multiple_of((n_chunks - 1 - i) * c, c)
        q = q_ref[0, pl.ds(r0, c), :]
        k = k_ref[0, pl.ds(r0, c), :]
        v = v_ref[0, pl.ds(r0, c), :]
        ob_ref[pl.ds(r0, c), :] = _dot(scaled(q, 2), sb_ref[...].astype(BF16))
        sb_ref[...] = sb_ref[...] * cdb + _dot_tn(scaled(k, 3), v)
        return carry

    lax.fori_loop(0, n_chunks, bwd_step, 0)

    def fwd_step(i, carry):
        r0 = pl.multiple_of(i * c, c)
        q = q_ref[0, pl.ds(r0, c), :]
        k = k_ref[0, pl.ds(r0, c), :]
        v = v_ref[0, pl.ds(r0, c), :]
        s = (_dot_nt(q, k) * dmat_ref[...]).astype(BF16)
        o = _dot(s, v) + _dot(scaled(q, 0), sf_ref[...].astype(BF16)) + ob_ref[pl.ds(r0, c), :]
        o_ref[0, pl.ds(r0, c), :] = _rms(o).astype(BF16)
        sf_ref[...] = sf_ref[...] * cdf + _dot_tn(scaled(k, 1), v)
        return carry

    lax.fori_loop(0, n_chunks, fwd_step, 0)


def _retention(lg, rq, rk, rv, rk_ctx, rv_ctx):
    b, l, _ = rq.shape
    lc = rk_ctx.shape[1]
    c = RET_CHUNK
    assert l % c == 0 and lc % c == 0
    qk = lambda n: pl.BlockSpec((1, n, RET_QK_DIM), lambda i, h: (i, 0, h))
    vv = lambda n: pl.BlockSpec((1, n, RET_V_DIM), lambda i, h: (i, 0, h))
    return pl.pallas_call(
        functools.partial(_retention_kernel, l // c, lc // c),
        out_shape=jax.ShapeDtypeStruct((b, l, N_RET_HEADS * RET_V_DIM), BF16),
        grid=(b, N_RET_HEADS),
        in_specs=[pl.BlockSpec(memory_space=pltpu.SMEM), qk(l), qk(l), vv(l), qk(lc), vv(lc)],
        out_specs=vv(l),
        scratch_shapes=[pltpu.VMEM((l, RET_V_DIM), F32),
                        pltpu.VMEM((RET_QK_DIM, RET_V_DIM), F32),
                        pltpu.VMEM((RET_QK_DIM, RET_V_DIM), F32),
                        pltpu.VMEM((c, c), F32),
                        pltpu.VMEM((4, c, RET_QK_DIM), F32)],
        compiler_params=_cparams(("arbitrary", "arbitrary")),
        name="retention",
    )(lg, rq, rk, rv, rk_ctx, rv_ctx)


def _attention_kernel(q_ref, kl_ref, kc_ref, vl_ref, vc_ref, o_ref):
    q = q_ref[0]
    s_l = _dot_nt(q, kl_ref[0])
    s_c = _dot_nt(q, kc_ref[0])
    m = jnp.maximum(jnp.max(s_l, axis=-1, keepdims=True), jnp.max(s_c, axis=-1, keepdims=True))
    p_l = jnp.exp(s_l - m)
    p_c = jnp.exp(s_c - m)
    denom = jnp.sum(p_l, axis=-1, keepdims=True) + jnp.sum(p_c, axis=-1, keepdims=True)
    o = _dot(p_l.astype(BF16), vl_ref[0]) + _dot(p_c.astype(BF16), vc_ref[0])
    o_ref[0] = (o / denom).astype(BF16)


def _attention(q, k_lat, k_ctx, v_lat, v_ctx):
    b, l, _ = q.shape
    lc = k_ctx.shape[1]
    tq = 256
    return pl.pallas_call(
        _attention_kernel,
        out_shape=jax.ShapeDtypeStruct((b, l, N_MLA_HEADS * MLA_V_DIM), BF16),
        grid=(b, N_MLA_HEADS, l // tq),
        in_specs=[pl.BlockSpec((1, tq, MLA_HEAD_PAD), lambda i, h, j: (i, j, h)),
                  pl.BlockSpec((1, l, MLA_HEAD_PAD), lambda i, h, j: (i, 0, h)),
                  pl.BlockSpec((1, lc, MLA_HEAD_PAD), lambda i, h, j: (i, 0, h)),
                  pl.BlockSpec((1, l, MLA_V_DIM), lambda i, h, j: (i, 0, h)),
                  pl.BlockSpec((1, lc, MLA_V_DIM), lambda i, h, j: (i, 0, h))],
        out_specs=pl.BlockSpec((1, tq, MLA_V_DIM), lambda i, h, j: (i, j, h)),
        compiler_params=_cparams(("arbitrary", "arbitrary", "arbitrary")),
        name="attention",
    )(q, k_lat, k_ctx, v_lat, v_ctx)


def _merge_kernel(x_ref, ret_ref, sgate_ref, att_ref, sgr_ref, sgm_ref, gate1_ref, shift2_ref, scale2_ref,
                  g2_ref, wr_ref, wm_ref, wo_ref, wrt_ref, brt_ref, x1_ref, h2_ref, logit_ref):
    r = (ret_ref[0].astype(F32) * sgate_ref[0].astype(F32)).astype(BF16)
    merged = (sgr_ref[0].astype(F32) * _dot(r, wr_ref[...])
              + sgm_ref[0].astype(F32) * _dot(att_ref[0], wm_ref[...]))
    y = _dot(merged.astype(BF16), wo_ref[...])
    x1 = x_ref[0] + gate1_ref[0] * y
    x1_ref[0] = x1
    h2 = _norm_mod(x1, g2_ref[...], shift2_ref[0], scale2_ref[0]).astype(BF16)
    h2_ref[0] = h2
    logit_ref[0] = _dot(h2, wrt_ref[...]) + brt_ref[...]


def _merge(x, ret_o, sgate, att_o, sgr, sgm, gate1, shift2, scale2, g2, wr, wm, wo, wrt, brt):
    b, l, d = x.shape
    tm = 256
    tok = lambda n: pl.BlockSpec((1, tm, n), lambda i, j: (i, j, 0))
    vec = pl.BlockSpec((1, 1, d), lambda i, j: (i, 0, 0))
    return pl.pallas_call(
        _merge_kernel,
        out_shape=[jax.ShapeDtypeStruct((b, l, d), F32), jax.ShapeDtypeStruct((b, l, d), BF16),
                   jax.ShapeDtypeStruct((b, l, LANES), F32)],
        grid=(b, l // tm),
        in_specs=[tok(d), tok(ret_o.shape[2]), tok(sgate.shape[2]), tok(att_o.shape[2]), tok(d), tok(d),
                  vec, vec, vec, _resident((1, d)), _resident(wr.shape), _resident(wm.shape),
                  _resident(wo.shape), _resident(wrt.shape), _resident(brt.shape)],
        out_specs=[tok(d), tok(d), tok(LANES)],
        compiler_params=_cparams(("arbitrary", "arbitrary")),
        name="merge",
    )(x, ret_o, sgate, att_o, sgr, sgm, gate1, shift2, scale2, g2, wr, wm, wo, wrt, brt)


EXPERT_BLOCK = 256


def _expert_kernel(be_ref, nb_ref, x_ref, wgu_ref, bgu_ref, wd_ref, bd_ref, y_ref):
    @pl.when(pl.program_id(0) < nb_ref[0])
    def _():
        f = wd_ref.shape[1]
        x = x_ref[...]
        gate = _dot(x, wgu_ref[0, :, :f]) + bgu_ref[0, :, :f]
        up = _dot(x, wgu_ref[0, :, f:]) + bgu_ref[0, :, f:]
        gate = jnp.minimum(gate, SWIGLU_LIMIT)
        up = jnp.clip(up, -SWIGLU_LIMIT, SWIGLU_LIMIT)
        glu = gate * _sigmoid(SWIGLU_ALPHA * gate)
        act = ((up + 1.0) * glu).astype(BF16)
        y_ref[...] = _dot(act, wd_ref[0]) + bd_ref[0]


def _experts(block_e, n_used, xs, w_gu, b_gu, w_down, b_down):
    n_rows, d = xs.shape
    e, _, f2 = w_gu.shape
    f = f2 // 2
    bm = EXPERT_BLOCK
    return pl.pallas_call(
        _expert_kernel,
        out_shape=jax.ShapeDtypeStruct((n_rows, d), F32),
        grid_spec=pltpu.PrefetchScalarGridSpec(
            num_scalar_prefetch=2,
            grid=(n_rows // bm,),
            in_specs=[pl.BlockSpec((bm, d), lambda i, be, nb: (i, 0)),
                      pl.BlockSpec((1, d, f2), lambda i, be, nb: (be[i], 0, 0)),
                      pl.BlockSpec((1, 1, f2), lambda i, be, nb: (be[i], 0, 0)),
                      pl.BlockSpec((1, f, d), lambda i, be, nb: (be[i], 0, 0)),
                      pl.BlockSpec((1, 1, d), lambda i, be, nb: (be[i], 0, 0))],
            out_specs=pl.BlockSpec((bm, d), lambda i, be, nb: (i, 0))),
        compiler_params=_cparams(("arbitrary",)),
        name="experts",
    )(block_e, n_used, xs, w_gu, b_gu.reshape(e, 1, f2), w_down, b_down.reshape(e, 1, d))


def _route(logits, h2_flat):
    n_tok = logits.shape[0]
    bm = EXPERT_BLOCK
    top_val, top_idx = lax.top_k(logits, TOP_K)
    gates = jax.nn.softmax(top_val, axis=-1)
    nk = n_tok * TOP_K
    flat_e = top_idx.reshape(-1)
    order = jnp.argsort(flat_e)
    sorted_e = flat_e[order]
    sorted_tok = (order // TOP_K).astype(jnp.int32)
    counts = jnp.zeros((N_EXPERTS,), jnp.int32).at[flat_e].add(1)
    padded = (counts + bm - 1) // bm * bm
    start = jnp.cumsum(counts) - counts
    pad_end = jnp.cumsum(padded)
    pad_start = pad_end - padded
    dest = pad_start[sorted_e] + (jnp.arange(nk, dtype=jnp.int32) - start[sorted_e])
    n_rows = nk + N_EXPERTS * bm
    n_blocks = n_rows // bm
    src_tok = jnp.zeros((n_rows,), jnp.int32).at[dest].set(sorted_tok)
    row_of_flat = jnp.zeros((nk,), jnp.int32).at[order].set(dest)
    block_e = jnp.minimum(jnp.searchsorted(pad_end, jnp.arange(n_blocks, dtype=jnp.int32) * bm, side='right'),
                          N_EXPERTS - 1).astype(jnp.int32)
    n_used = (pad_end[-1] // bm).astype(jnp.int32).reshape(1)
    return gates, src_tok, row_of_flat, block_e, n_used


def _combine_kernel(x1_ref, y_ref, gates_ref, gate2_ref, gf_ref, o_ref):
    d = x1_ref.shape[2]
    gates = gates_ref[0]
    moe = gates[:, 0:1] * y_ref[0, :, 0:d]
    for kk in range(1, TOP_K):
        moe = moe + gates[:, kk:kk + 1] * y_ref[0, :, kk * d:(kk + 1) * d]
    x2 = x1_ref[0] + gate2_ref[0] * moe
    o_ref[0] = _rms(x2) * gf_ref[...]


def _combine(x1, yg, gates, gate2, gf):
    b, l, d = x1.shape
    tm = 256
    return pl.pallas_call(
        _combine_kernel,
        out_shape=jax.ShapeDtypeStruct((b, l, d), F32),
        grid=(b, l // tm),
        in_specs=[pl.BlockSpec((1, tm, d), lambda i, j: (i, j, 0)),
                  pl.BlockSpec((1, tm, TOP_K * d), lambda i, j: (i, j, 0)),
                  pl.BlockSpec((1, tm, TOP_K), lambda i, j: (i, j, 0)),
                  pl.BlockSpec((1, 1, d), lambda i, j: (i, 0, 0)),
                  _resident((1, d))],
        out_specs=pl.BlockSpec((1, tm, d), lambda i, j: (i, j, 0)),
        compiler_params=_cparams(("arbitrary", "arbitrary")),
        name="combine",
    )(x1, yg, gates, gate2, gf)


def kernel(x, c, ctx, c_ctx, norm1_g, norm2_g, ada_w, ada_b, w_in, ret_decay_fwd, ret_decay_bwd, mla_q_norm_g, mla_w_uq, mla_kv_norm_g, mla_w_ukv, w_branch_ret, w_branch_mla, w_out, router_w, router_b, exp_w_gu, exp_b_gu, exp_w_down, exp_b_down, final_norm_g):
    depth = norm1_g.shape[0]
    assert depth == 1, "single-layer block"
    b, l, d = x.shape
    qk_scale = float((MLA_QK_NOPE + MLA_QK_ROPE) ** -0.5)

    n_mod = b + 1
    rows = -(-n_mod // 8) * 8
    cvec = jnp.concatenate([c, c_ctx[None, :], jnp.zeros((rows - n_mod, d), F32)], axis=0)
    mod = _ada_mod(cvec, ada_w[0], ada_b[0])
    m_lat = [mod[:b, i * d:(i + 1) * d].reshape(b, 1, d) for i in range(6)]
    m_ctx = [mod[b:b + 1, i * d:(i + 1) * d] for i in range(2)]

    w_lat, w_ctx, uq, ukv = _prep_weights(w_in[0], mla_w_uq[0], mla_w_ukv[0])
    tret, tmla = _rope_tables(l, qk_scale)
    g1 = norm1_g[0].reshape(1, d)
    gq = mla_q_norm_g[0].reshape(1, MLA_Q_RANK)
    gkv = mla_kv_norm_g[0].reshape(1, MLA_KV_RANK)

    rq, rk, rv, sgate, q, k_lat, v_lat, sgr, sgm = _inproj_lat(
        x, m_lat[0], m_lat[1], g1, w_lat, uq, ukv, gq, gkv, tret, tmla, qk_scale)
    rk_ctx, rv_ctx, k_ctx, v_ctx = _inproj_ctx(ctx, m_ctx[0], m_ctx[1], g1, w_ctx, ukv, gkv)

    lg = jnp.stack([jax.nn.log_sigmoid(ret_decay_fwd[0].astype(F32)),
                    jax.nn.log_sigmoid(ret_decay_bwd[0].astype(F32))])
    ret_o = _retention(lg, rq, rk, rv, rk_ctx, rv_ctx)
    att_o = _attention(q, k_lat, k_ctx, v_lat, v_ctx)

    wrt = jnp.pad(router_w[0], ((0, 0), (0, LANES - N_EXPERTS))).astype(BF16)
    brt = jnp.pad(router_b[0], (0, LANES - N_EXPERTS)).reshape(1, LANES)
    x1, h2, logits = _merge(x, ret_o, sgate, att_o, sgr, sgm, m_lat[2], m_lat[3], m_lat[4],
                            norm2_g[0].reshape(1, d), w_branch_ret[0].astype(BF16),
                            w_branch_mla[0].astype(BF16), w_out[0].astype(BF16), wrt, brt)

    n_tok = b * l
    h2_flat = h2.reshape(n_tok, d)
    gates, src_tok, row_of_flat, block_e, n_used = _route(logits.reshape(n_tok, LANES)[:, :N_EXPERTS], h2_flat)
    xs = h2_flat[src_tok]
    ys = _experts(block_e, n_used, xs, exp_w_gu[0].astype(BF16), exp_b_gu[0],
                  exp_w_down[0].astype(BF16), exp_b_down[0])
    yg = ys[row_of_flat].reshape(b, l, TOP_K * d)
    return _combine(x1, yg, gates.reshape(b, l, TOP_K), m_lat[5], final_norm_g.reshape(1, d))
```

```python
import functools

import numpy as np
import jax
import jax.numpy as jnp
from jax import lax
from jax.experimental import pallas as pl
from jax.experimental.pallas import tpu as pltpu

GRID_W = 64
N_RET_HEADS = 4
RET_QK_DIM = 256
RET_V_DIM = 512
N_MLA_HEADS = 8
MLA_Q_RANK = 384
MLA_KV_RANK = 256
MLA_QK_NOPE = 128
MLA_QK_ROPE = 64
MLA_V_DIM = 128
N_EXPERTS = 32
TOP_K = 4
SWIGLU_LIMIT = 7.0
SWIGLU_ALPHA = 1.702
ROPE_BASE = 10000.0
EPS = 1e-6

LANES = 128
MLA_HEAD_PAD = 2 * LANES
RET_CHUNK = 256
VMEM_LIMIT = 56 * 1024 * 1024

F32 = jnp.float32
BF16 = jnp.bfloat16


def _cparams(sem):
    return pltpu.CompilerParams(dimension_semantics=sem, vmem_limit_bytes=VMEM_LIMIT)


def _resident(shape):
    nd = len(shape)
    return pl.BlockSpec(shape, lambda *_: (0,) * nd, pipeline_mode=pl.Buffered(1))


def _dot(a, b):
    return jnp.dot(a, b, preferred_element_type=F32)


def _dot_nt(a, b):
    return lax.dot_general(a, b, (((1,), (1,)), ((), ())), preferred_element_type=F32)


def _dot_tn(a, b):
    return lax.dot_general(a, b, (((0,), (0,)), ((), ())), preferred_element_type=F32)


def _rms(x):
    return x * lax.rsqrt(jnp.mean(x * x, axis=-1, keepdims=True) + EPS)


def _sigmoid(x):
    return 1.0 / (1.0 + jnp.exp(-x))


def _ada_kernel(c_ref, w_ref, b_ref, o_ref):
    c = c_ref[...]
    s = c * _sigmoid(c)
    o_ref[...] = jnp.dot(s, w_ref[...], preferred_element_type=F32,
                         precision=lax.Precision.HIGHEST) + b_ref[...]


def _ada_mod(cvec, w, b):
    rows, d = cvec.shape
    n = w.shape[1]
    tn = 1024
    return pl.pallas_call(
        _ada_kernel,
        out_shape=jax.ShapeDtypeStruct((rows, n), F32),
        grid=(n // tn,),
        in_specs=[pl.BlockSpec((rows, d), lambda j: (0, 0)),
                  pl.BlockSpec((d, tn), lambda j: (0, j)),
                  pl.BlockSpec((1, tn), lambda j: (0, j))],
        out_specs=pl.BlockSpec((rows, tn), lambda j: (0, j)),
        compiler_params=_cparams(("arbitrary",)),
        name="ada_mod",
    )(cvec, w, b.reshape(1, n))


def _rope_tables(seq_len, qk_scale):
    pos = np.arange(seq_len)
    rows = (pos // GRID_W).astype(np.float32)
    cols = (pos % GRID_W).astype(np.float32)

    def angles(p, half):
        freqs = (np.float32(ROPE_BASE) ** (-np.arange(half, dtype=np.float32) / np.float32(half))).astype(np.float32)
        return (p[:, None] * freqs[None, :]).astype(np.float32)

    a_r, a_c = angles(rows, 64), angles(cols, 64)
    ret = np.concatenate([np.cos(a_r), np.cos(a_r), np.cos(a_c), np.cos(a_c),
                          -np.sin(a_r), np.sin(a_r), -np.sin(a_c), np.sin(a_c)], axis=1)
    b_r, b_c = angles(rows, 16), angles(cols, 16)
    z32 = np.zeros((seq_len, 32), np.float32)
    z64 = np.zeros((seq_len, 64), np.float32)
    cos = np.concatenate([np.cos(b_r), np.cos(b_c), np.cos(b_r), np.cos(b_c), z64], axis=1)
    sin_up = np.concatenate([-np.sin(b_r), -np.sin(b_c), z32, z64], axis=1)
    sin_dn = np.concatenate([z32, np.sin(b_r), np.sin(b_c), z64], axis=1)
    mla_k = np.concatenate([cos, sin_up, sin_dn], axis=1)
    mla = np.concatenate([mla_k, mla_k * np.float32(qk_scale)], axis=1)
    return jnp.asarray(ret, F32), jnp.asarray(mla, F32)


def _rope_ret(t, tab, parity):
    cos = tab[:, parity * LANES:(parity + 1) * LANES]
    sin = tab[:, (2 + parity) * LANES:(3 + parity) * LANES]
    return t * cos + pltpu.roll(t, 64, 1) * sin


def _rope_mla(t, tab, base):
    cos = tab[:, base:base + LANES]
    sin_up = tab[:, base + LANES:base + 2 * LANES]
    sin_dn = tab[:, base + 2 * LANES:base + 3 * LANES]
    return t * cos + pltpu.roll(t, 96, 1) * sin_up + pltpu.roll(t, 32, 1) * sin_dn


_C_Q, _C_K, _C_V, _C_G = 0, 1024, 2048, 4096
_C_CQ, _C_CKV, _C_KR, _C_GR, _C_GM, _C_END = 6144, 6528, 6784, 6912, 7936, 8960
_X_K, _X_V, _X_CKV, _X_KR, _X_END = 0, 1024, 3072, 3328, 3456


def _norm_mod(x, g, shift, scale):
    return (_rms(x) * g) * (1.0 + scale) + shift


def _mla_kv(ckv_acc, kr, gkv_ref, wuk_ref, wuvt_ref, k_ref, vt_ref):
    ckvn = (_rms(ckv_acc) * gkv_ref[...]).astype(BF16)
    kn = _dot(ckvn, wuk_ref[...])
    krb = kr.astype(BF16)
    for hh in range(N_MLA_HEADS):
        k_ref[0, :, hh * MLA_HEAD_PAD:hh * MLA_HEAD_PAD + LANES] = kn[:, hh * LANES:(hh + 1) * LANES].astype(BF16)
        k_ref[0, :, hh * MLA_HEAD_PAD + LANES:(hh + 1) * MLA_HEAD_PAD] = krb
    vt_ref[0] = _dot_nt(wuvt_ref[...], ckvn).astype(BF16)


def _inproj_lat_kernel(qk_scale, x_ref, shift_ref, scale_ref, g_ref, w_ref, wuq_ref, wuk_ref, wuvt_ref, gq_ref,
                       gkv_ref, tret_ref, tmla_ref,
                       rq_ref, rk_ref, rv_ref, sgate_ref, q_ref, k_ref, vt_ref, sgr_ref, sgm_ref):
    h = _norm_mod(x_ref[0], g_ref[...], shift_ref[0], scale_ref[0]).astype(BF16)
    tret = tret_ref[...]
    tmla = tmla_ref[...]

    def proj(c0, n):
        return _dot(h, w_ref[:, c0:c0 + n])

    for base, out in ((_C_Q, rq_ref), (_C_K, rk_ref)):
        for j in range(2):
            acc = proj(base + j * 512, 512)
            for blk in range(4):
                t = acc[:, blk * LANES:(blk + 1) * LANES]
                col = j * 512 + blk * LANES
                out[0, :, col:col + LANES] = _rope_ret(t, tret, blk % 2).astype(BF16)
    for j in range(4):
        rv_ref[0, :, j * 512:(j + 1) * 512] = proj(_C_V + j * 512, 512).astype(BF16)
    for j in range(4):
        a = proj(_C_G + j * 512, 512)
        sgate_ref[0, :, j * 512:(j + 1) * 512] = (a * _sigmoid(a)).astype(BF16)
    for j in range(2):
        a = proj(_C_GR + j * 512, 512)
        sgr_ref[0, :, j * 512:(j + 1) * 512] = _sigmoid(a).astype(BF16)
    for j in range(2):
        a = proj(_C_GM + j * 512, 512)
        sgm_ref[0, :, j * 512:(j + 1) * 512] = _sigmoid(a).astype(BF16)

    cqn = (_rms(proj(_C_CQ, MLA_Q_RANK)) * gq_ref[...]).astype(BF16)
    for j in range(4):
        acc = _dot(cqn, wuq_ref[:, j * 512:(j + 1) * 512])
        for blk in range(4):
            t = acc[:, blk * LANES:(blk + 1) * LANES]
            col = j * 512 + blk * LANES
            if blk % 2 == 0:
                q_ref[0, :, col:col + LANES] = (t * qk_scale).astype(BF16)
            else:
                q_ref[0, :, col:col + LANES] = _rope_mla(t, tmla, 3 * LANES).astype(BF16)
    kr = _rope_mla(proj(_C_KR, LANES), tmla, 0)
    _mla_kv(proj(_C_CKV, MLA_KV_RANK), kr, gkv_ref, wuk_ref, wuvt_ref, k_ref, vt_ref)


def _inproj_ctx_kernel(x_ref, shift_ref, scale_ref, g_ref, w_ref, wuk_ref, wuvt_ref, gkv_ref,
                       rk_ref, rv_ref, k_ref, vt_ref):
    h = _norm_mod(x_ref[0], g_ref[...], shift_ref[...], scale_ref[...]).astype(BF16)

    def proj(c0, n):
        return _dot(h, w_ref[:, c0:c0 + n])

    for j in range(2):
        rk_ref[0, :, j * 512:(j + 1) * 512] = proj(_X_K + j * 512, 512).astype(BF16)
    for j in range(4):
        rv_ref[0, :, j * 512:(j + 1) * 512] = proj(_X_V + j * 512, 512).astype(BF16)
    _mla_kv(proj(_X_CKV, MLA_KV_RANK), proj(_X_KR, LANES), gkv_ref, wuk_ref, wuvt_ref, k_ref, vt_ref)


def _prep_weights(w_in, w_uq, w_ukv):
    nq = N_RET_HEADS * RET_QK_DIM
    nv = N_RET_HEADS * RET_V_DIM
    d = w_in.shape[0]
    sizes = (nq, nq, nv, nv, MLA_Q_RANK, MLA_KV_RANK, MLA_QK_ROPE, d, d)
    offs = np.cumsum((0,) + sizes)
    wq, wk, wv, wg, wcq, wckv, wkr, wgr, wgm = [w_in[:, offs[i]:offs[i + 1]] for i in range(9)]
    wk = wk * (RET_QK_DIM ** -0.5)
    perm = np.concatenate([np.arange(0, 16), np.arange(32, 48), np.arange(16, 32), np.arange(48, 64)])
    wkr = jnp.pad(wkr[:, perm], ((0, 0), (0, LANES - MLA_QK_ROPE)))
    w_lat = jnp.concatenate([wq, wk, wv, wg, wcq, wckv, wkr, wgr, wgm], axis=1).astype(BF16)
    w_ctx = jnp.concatenate([wk, wv, wckv, wkr], axis=1).astype(BF16)
    uq = w_uq.reshape(MLA_Q_RANK, N_MLA_HEADS, MLA_QK_NOPE + MLA_QK_ROPE)
    uq = jnp.concatenate([uq[:, :, :MLA_QK_NOPE], uq[:, :, MLA_QK_NOPE:][:, :, perm],
                          jnp.zeros((MLA_Q_RANK, N_MLA_HEADS, LANES - MLA_QK_ROPE), w_uq.dtype)], axis=2)
    uq = uq.reshape(MLA_Q_RANK, N_MLA_HEADS * MLA_HEAD_PAD).astype(BF16)
    ukv = w_ukv.reshape(MLA_KV_RANK, N_MLA_HEADS, MLA_QK_NOPE + MLA_V_DIM)
    uk = ukv[:, :, :MLA_QK_NOPE].reshape(MLA_KV_RANK, -1).astype(BF16)
    uvt = ukv[:, :, MLA_QK_NOPE:].reshape(MLA_KV_RANK, -1).T.astype(BF16)
    return w_lat, w_ctx, uq, uk, uvt


def _tok_spec(tm, n):
    return pl.BlockSpec((1, tm, n), lambda i, j: (i, j, 0))


def _vt_spec(tm):
    return pl.BlockSpec((1, N_MLA_HEADS * MLA_V_DIM, tm), lambda i, j: (i, 0, j))


def _inproj_lat(x, shift, scale, g, w_lat, uq, uk, uvt, gq, gkv, tret, tmla, qk_scale):
    b, l, d = x.shape
    tm = 256
    vec = pl.BlockSpec((1, 1, d), lambda i, j: (i, 0, 0))
    out_w = (1024, 1024, 2048, 2048, 2048, 2048, None, 1024, 1024)
    nvt = N_MLA_HEADS * MLA_V_DIM
    return pl.pallas_call(
        functools.partial(_inproj_lat_kernel, qk_scale),
        out_shape=[jax.ShapeDtypeStruct((b, nvt, l) if n is None else (b, l, n), BF16) for n in out_w],
        grid=(b, l // tm),
        in_specs=[_tok_spec(tm, d), vec, vec, _resident((1, d)), _resident(w_lat.shape), _resident(uq.shape),
                  _resident(uk.shape), _resident(uvt.shape), _resident((1, MLA_Q_RANK)),
                  _resident((1, MLA_KV_RANK)),
                  pl.BlockSpec((tm, tret.shape[1]), lambda i, j: (j, 0)),
                  pl.BlockSpec((tm, tmla.shape[1]), lambda i, j: (j, 0))],
        out_specs=[_vt_spec(tm) if n is None else _tok_spec(tm, n) for n in out_w],
        compiler_params=_cparams(("arbitrary", "arbitrary")),
        name="inproj_lat",
    )(x, shift, scale, g, w_lat, uq, uk, uvt, gq, gkv, tret, tmla)


def _inproj_ctx(ctx, shift, scale, g, w_ctx, uk, uvt, gkv):
    b, l, d = ctx.shape
    tm = 256
    out_w = (1024, 2048, 2048, None)
    nvt = N_MLA_HEADS * MLA_V_DIM
    return pl.pallas_call(
        _inproj_ctx_kernel,
        out_shape=[jax.ShapeDtypeStruct((b, nvt, l) if n is None else (b, l, n), BF16) for n in out_w],
        grid=(b, l // tm),
        in_specs=[_tok_spec(tm, d), _resident((1, d)), _resident((1, d)), _resident((1, d)),
                  _resident(w_ctx.shape), _resident(uk.shape), _resident(uvt.shape),
                  _resident((1, MLA_KV_RANK))],
        out_specs=[_vt_spec(tm) if n is None else _tok_spec(tm, n) for n in out_w],
        compiler_params=_cparams(("arbitrary", "arbitrary")),
        name="inproj_ctx",
    )(ctx, shift, scale, g, w_ctx, uk, uvt, gkv)


def _retention_kernel(n_chunks, n_ctx_chunks, lg_ref, q_ref, k_ref, v_ref, kc_ref, vc_ref, o_ref,
                      ob_ref, sf_ref, sb_ref, dmat_ref, dec_ref):
    c = RET_CHUNK
    head = pl.program_id(1)
    lgf = lg_ref[0, head]
    lgb = lg_ref[1, head]
    ri = lax.broadcasted_iota(jnp.int32, (c, c), 0).astype(F32)
    ci = lax.broadcasted_iota(jnp.int32, (c, c), 1).astype(F32)
    diff = ri - ci
    dmat_ref[...] = jnp.where(diff >= 0, jnp.exp(lgf * jnp.maximum(diff, 0.0)),
                              jnp.exp(lgb * jnp.maximum(-diff, 0.0)))
    rk = lax.broadcasted_iota(jnp.int32, (c, RET_QK_DIM), 0).astype(F32)
    dec_ref[0] = jnp.exp(lgf * (rk + 1.0))
    dec_ref[1] = jnp.exp(lgf * (c - 1.0 - rk))
    dec_ref[2] = jnp.exp(lgb * (c - rk))
    dec_ref[3] = jnp.exp(lgb * rk)
    cdf = jnp.exp(jnp.full((1, RET_V_DIM), lgf * c, F32))
    cdb = jnp.exp(jnp.full((1, RET_V_DIM), lgb * c, F32))

    def scaled(t, which):
        return (t.astype(F32) * dec_ref[which]).astype(BF16)

    sf_ref[...] = jnp.zeros_like(sf_ref)
    sb_ref[...] = jnp.zeros_like(sb_ref)
    for n in range(n_ctx_chunks):
        sl = slice(n * c, (n + 1) * c)
        sf_ref[...] = sf_ref[...] * cdf + _dot_tn(scaled(kc_ref[0, sl, :], 1), vc_ref[0, sl, :])
    for n in reversed(range(n_ctx_chunks)):
        sl = slice(n * c, (n + 1) * c)
        sb_ref[...] = sb_ref[...] * cdb + _dot_tn(scaled(kc_ref[0, sl, :], 3), vc_ref[0, sl, :])

    def bwd_step(i, carry):
        r0 = pl.multiple_of((n_chunks - 1 - i) * c, c)
        q = q_ref[0, pl.ds(r0, c), :]
        k = k_ref[0, pl.ds(r0, c), :]
        v = v_ref[0, pl.ds(r0, c), :]
        ob_ref[pl.ds(r0, c), :] = _dot(scaled(q, 2), sb_ref[...].astype(BF16))
        sb_ref[...] = sb_ref[...] * cdb + _dot_tn(scaled(k, 3), v)
        return carry

    lax.fori_loop(0, n_chunks, bwd_step, 0)

    def fwd_step(i, carry):
        r0 = pl.multiple_of(i * c, c)
        q = q_ref[0, pl.ds(r0, c), :]
        k = k_ref[0, pl.ds(r0, c), :]
        v = v_ref[0, pl.ds(r0, c), :]
        s = (_dot_nt(q, k) * dmat_ref[...]).astype(BF16)
        o = _dot(s, v) + _dot(scaled(q, 0), sf_ref[...].astype(BF16)) + ob_ref[pl.ds(r0, c), :]
        o_ref[0, pl.ds(r0, c), :] = _rms(o).astype(BF16)
        sf_ref[...] = sf_ref[...] * cdf + _dot_tn(scaled(k, 1), v)
        return carry

    lax.fori_loop(0, n_chunks, fwd_step, 0)


def _retention(lg, rq, rk, rv, rk_ctx, rv_ctx):
    b, l, _ = rq.shape
    lc = rk_ctx.shape[1]
    c = RET_CHUNK
    assert l % c == 0 and lc % c == 0
    qk = lambda n: pl.BlockSpec((1, n, RET_QK_DIM), lambda i, h: (i, 0, h))
    vv = lambda n: pl.BlockSpec((1, n, RET_V_DIM), lambda i, h: (i, 0, h))
    return pl.pallas_call(
        functools.partial(_retention_kernel, l // c, lc // c),
        out_shape=jax.ShapeDtypeStruct((b, l, N_RET_HEADS * RET_V_DIM), BF16),
        grid=(b, N_RET_HEADS),
        in_specs=[pl.BlockSpec(memory_space=pltpu.SMEM), qk(l), qk(l), vv(l), qk(lc), vv(lc)],
        out_specs=vv(l),
        scratch_shapes=[pltpu.VMEM((l, RET_V_DIM), F32),
                        pltpu.VMEM((RET_QK_DIM, RET_V_DIM), F32),
                        pltpu.VMEM((RET_QK_DIM, RET_V_DIM), F32),
                        pltpu.VMEM((c, c), F32),
                        pltpu.VMEM((4, c, RET_QK_DIM), F32)],
        compiler_params=_cparams(("arbitrary", "arbitrary")),
        name="retention",
    )(lg, rq, rk, rv, rk_ctx, rv_ctx)


ATTN_CHAIN = 256
ATTN_CHAINS = 2


ATTN_KEY_BLOCK = 512


def _attention_kernel(q_ref, kl_ref, kc_ref, vtl_ref, vtc_ref, o_ref):
    l = kl_ref.shape[1]
    lc = kc_ref.shape[1]
    blocks = [(kc_ref, vtc_ref, 0, lc)]
    blocks += [(kl_ref, vtl_ref, r0, ATTN_KEY_BLOCK) for r0 in range(0, l, ATTN_KEY_BLOCK)]
    for ch in range(ATTN_CHAINS):
        rows = slice(ch * ATTN_CHAIN, (ch + 1) * ATTN_CHAIN)
        q = q_ref[0, rows, :]
        m = denom = acc = None
        for k_ref, vt_ref, r0, n in blocks:
            s = _dot_nt(k_ref[0, r0:r0 + n, :], q)
            m_blk = jnp.max(s, axis=0, keepdims=True)
            if m is None:
                m = m_blk
                p = jnp.exp2(s - m)
                denom = jnp.sum(p, axis=0, keepdims=True)
                acc = _dot(vt_ref[0, :, r0:r0 + n], p.astype(BF16))
            else:
                m_new = jnp.maximum(m, m_blk)
                alpha = jnp.exp2(m - m_new)
                p = jnp.exp2(s - m_new)
                denom = denom * alpha + jnp.sum(p, axis=0, keepdims=True)
                acc = acc * alpha + _dot(vt_ref[0, :, r0:r0 + n], p.astype(BF16))
                m = m_new
        o_ref[0, rows, :] = (acc / denom).T.astype(BF16)


def _attention(q, k_lat, k_ctx, vt_lat, vt_ctx):
    b, l, _ = q.shape
    lc = k_ctx.shape[1]
    tq = ATTN_CHAIN * ATTN_CHAINS
    return pl.pallas_call(
        _attention_kernel,
        out_shape=jax.ShapeDtypeStruct((b, l, N_MLA_HEADS * MLA_V_DIM), BF16),
        grid=(b, N_MLA_HEADS, l // tq),
        in_specs=[pl.BlockSpec((1, tq, MLA_HEAD_PAD), lambda i, h, j: (i, j, h)),
                  pl.BlockSpec((1, l, MLA_HEAD_PAD), lambda i, h, j: (i, 0, h)),
                  pl.BlockSpec((1, lc, MLA_HEAD_PAD), lambda i, h, j: (i, 0, h)),
                  pl.BlockSpec((1, MLA_V_DIM, l), lambda i, h, j: (i, h, 0)),
                  pl.BlockSpec((1, MLA_V_DIM, lc), lambda i, h, j: (i, h, 0))],
        out_specs=pl.BlockSpec((1, tq, MLA_V_DIM), lambda i, h, j: (i, j, h)),
        compiler_params=_cparams(("arbitrary", "arbitrary", "arbitrary")),
        name="attention",
    )(q, k_lat, k_ctx, vt_lat, vt_ctx)


def _merge_kernel(x_ref, ret_ref, sgate_ref, att_ref, sgr_ref, sgm_ref, gate1_ref, shift2_ref, scale2_ref,
                  g2_ref, wr_ref, wm_ref, wo_ref, wrt_ref, brt_ref, x1_ref, h2_ref, logit_ref):
    r = (ret_ref[0].astype(F32) * sgate_ref[0].astype(F32)).astype(BF16)
    merged = (sgr_ref[0].astype(F32) * _dot(r, wr_ref[...])
              + sgm_ref[0].astype(F32) * _dot(att_ref[0], wm_ref[...]))
    y = _dot(merged.astype(BF16), wo_ref[...])
    x1 = x_ref[0] + gate1_ref[0] * y
    x1_ref[0] = x1
    h2 = _norm_mod(x1, g2_ref[...], shift2_ref[0], scale2_ref[0]).astype(BF16)
    h2_ref[0] = h2
    logit_ref[0] = _dot(h2, wrt_ref[...]) + brt_ref[...]


def _merge(x, ret_o, sgate, att_o, sgr, sgm, gate1, shift2, scale2, g2, wr, wm, wo, wrt, brt):
    b, l, d = x.shape
    tm = 256
    tok = lambda n: pl.BlockSpec((1, tm, n), lambda i, j: (i, j, 0))
    vec = pl.BlockSpec((1, 1, d), lambda i, j: (i, 0, 0))
    return pl.pallas_call(
        _merge_kernel,
        out_shape=[jax.ShapeDtypeStruct((b, l, d), F32), jax.ShapeDtypeStruct((b, l, d), BF16),
                   jax.ShapeDtypeStruct((b, l, LANES), F32)],
        grid=(b, l // tm),
        in_specs=[tok(d), tok(ret_o.shape[2]), tok(sgate.shape[2]), tok(att_o.shape[2]), tok(d), tok(d),
                  vec, vec, vec, _resident((1, d)), _resident(wr.shape), _resident(wm.shape),
                  _resident(wo.shape), _resident(wrt.shape), _resident(brt.shape)],
        out_specs=[tok(d), tok(d), tok(LANES)],
        compiler_params=_cparams(("arbitrary", "arbitrary")),
        name="merge",
    )(x, ret_o, sgate, att_o, sgr, sgm, gate1, shift2, scale2, g2, wr, wm, wo, wrt, brt)


EXPERT_BLOCK = 256


def _expert_kernel(be_ref, nb_ref, x_ref, wgu_ref, bgu_ref, wd_ref, bd_ref, y_ref):
    @pl.when(pl.program_id(0) < nb_ref[0])
    def _():
        f = wd_ref.shape[1]
        x = x_ref[...]
        gate = _dot(x, wgu_ref[0, :, :f]) + bgu_ref[0, :, :f]
        up = _dot(x, wgu_ref[0, :, f:]) + bgu_ref[0, :, f:]
        gate = jnp.minimum(gate, SWIGLU_LIMIT)
        up = jnp.clip(up, -SWIGLU_LIMIT, SWIGLU_LIMIT)
        glu = gate * _sigmoid(SWIGLU_ALPHA * gate)
        act = ((up + 1.0) * glu).astype(BF16)
        y_ref[...] = _dot(act, wd_ref[0]) + bd_ref[0]


def _experts(block_e, n_used, xs, w_gu, b_gu, w_down, b_down):
    n_rows, d = xs.shape
    e, _, f2 = w_gu.shape
    f = f2 // 2
    bm = EXPERT_BLOCK
    return pl.pallas_call(
        _expert_kernel,
        out_shape=jax.ShapeDtypeStruct((n_rows, d), F32),
        grid_spec=pltpu.PrefetchScalarGridSpec(
            num_scalar_prefetch=2,
            grid=(n_rows // bm,),
            in_specs=[pl.BlockSpec((bm, d), lambda i, be, nb: (i, 0)),
                      pl.BlockSpec((1, d, f2), lambda i, be, nb: (be[i], 0, 0)),
                      pl.BlockSpec((1, 1, f2), lambda i, be, nb: (be[i], 0, 0)),
                      pl.BlockSpec((1, f, d), lambda i, be, nb: (be[i], 0, 0)),
                      pl.BlockSpec((1, 1, d), lambda i, be, nb: (be[i], 0, 0))],
            out_specs=pl.BlockSpec((bm, d), lambda i, be, nb: (i, 0))),
        compiler_params=_cparams(("arbitrary",)),
        name="experts",
    )(block_e, n_used, xs, w_gu, b_gu.reshape(e, 1, f2), w_down, b_down.reshape(e, 1, d))


def _route(logits, h2_flat):
    n_tok = logits.shape[0]
    bm = EXPERT_BLOCK
    top_val, top_idx = lax.top_k(logits, TOP_K)
    gates = jax.nn.softmax(top_val, axis=-1)
    nk = n_tok * TOP_K
    flat_e = top_idx.reshape(-1)
    order = jnp.argsort(flat_e)
    sorted_e = flat_e[order]
    sorted_tok = (order // TOP_K).astype(jnp.int32)
    counts = jnp.zeros((N_EXPERTS,), jnp.int32).at[flat_e].add(1)
    padded = (counts + bm - 1) // bm * bm
    start = jnp.cumsum(counts) - counts
    pad_end = jnp.cumsum(padded)
    pad_start = pad_end - padded
    dest = pad_start[sorted_e] + (jnp.arange(nk, dtype=jnp.int32) - start[sorted_e])
    n_rows = nk + N_EXPERTS * bm
    n_blocks = n_rows // bm
    src_tok = jnp.zeros((n_rows,), jnp.int32).at[dest].set(sorted_tok)
    row_of_flat = jnp.zeros((nk,), jnp.int32).at[order].set(dest)
    block_e = jnp.minimum(jnp.searchsorted(pad_end, jnp.arange(n_blocks, dtype=jnp.int32) * bm, side='right'),
                          N_EXPERTS - 1).astype(jnp.int32)
    n_used = (pad_end[-1] // bm).astype(jnp.int32).reshape(1)
    return gates, src_tok, row_of_flat, block_e, n_used


def _combine_kernel(x1_ref, y_ref, gates_ref, gate2_ref, gf_ref, o_ref):
    d = x1_ref.shape[2]
    gates = gates_ref[0]
    moe = gates[:, 0:1] * y_ref[0, :, 0:d]
    for kk in range(1, TOP_K):
        moe = moe + gates[:, kk:kk + 1] * y_ref[0, :, kk * d:(kk + 1) * d]
    x2 = x1_ref[0] + gate2_ref[0] * moe
    o_ref[0] = _rms(x2) * gf_ref[...]


def _combine(x1, yg, gates, gate2, gf):
    b, l, d = x1.shape
    tm = 256
    return pl.pallas_call(
        _combine_kernel,
        out_shape=jax.ShapeDtypeStruct((b, l, d), F32),
        grid=(b, l // tm),
        in_specs=[pl.BlockSpec((1, tm, d), lambda i, j: (i, j, 0)),
                  pl.BlockSpec((1, tm, TOP_K * d), lambda i, j: (i, j, 0)),
                  pl.BlockSpec((1, tm, TOP_K), lambda i, j: (i, j, 0)),
                  pl.BlockSpec((1, 1, d), lambda i, j: (i, 0, 0)),
                  _resident((1, d))],
        out_specs=pl.BlockSpec((1, tm, d), lambda i, j: (i, j, 0)),
        compiler_params=_cparams(("arbitrary", "arbitrary")),
        name="combine",
    )(x1, yg, gates, gate2, gf)


def kernel(x, c, ctx, c_ctx, norm1_g, norm2_g, ada_w, ada_b, w_in, ret_decay_fwd, ret_decay_bwd, mla_q_norm_g, mla_w_uq, mla_kv_norm_g, mla_w_ukv, w_branch_ret, w_branch_mla, w_out, router_w, router_b, exp_w_gu, exp_b_gu, exp_w_down, exp_b_down, final_norm_g):
    depth = norm1_g.shape[0]
    assert depth == 1, "single-layer block"
    b, l, d = x.shape
    qk_scale = float((MLA_QK_NOPE + MLA_QK_ROPE) ** -0.5 * np.log2(np.e))

    n_mod = b + 1
    rows = -(-n_mod // 8) * 8
    cvec = jnp.concatenate([c, c_ctx[None, :], jnp.zeros((rows - n_mod, d), F32)], axis=0)
    mod = _ada_mod(cvec, ada_w[0], ada_b[0])
    m_lat = [mod[:b, i * d:(i + 1) * d].reshape(b, 1, d) for i in range(6)]
    m_ctx = [mod[b:b + 1, i * d:(i + 1) * d] for i in range(2)]

    w_lat, w_ctx, uq, uk, uvt = _prep_weights(w_in[0], mla_w_uq[0], mla_w_ukv[0])
    tret, tmla = _rope_tables(l, qk_scale)
    g1 = norm1_g[0].reshape(1, d)
    gq = mla_q_norm_g[0].reshape(1, MLA_Q_RANK)
    gkv = mla_kv_norm_g[0].reshape(1, MLA_KV_RANK)

    rq, rk, rv, sgate, q, k_lat, vt_lat, sgr, sgm = _inproj_lat(
        x, m_lat[0], m_lat[1], g1, w_lat, uq, uk, uvt, gq, gkv, tret, tmla, qk_scale)
    rk_ctx, rv_ctx, k_ctx, vt_ctx = _inproj_ctx(ctx, m_ctx[0], m_ctx[1], g1, w_ctx, uk, uvt, gkv)

    lg = jnp.stack([jax.nn.log_sigmoid(ret_decay_fwd[0].astype(F32)),
                    jax.nn.log_sigmoid(ret_decay_bwd[0].astype(F32))])
    ret_o = _retention(lg, rq, rk, rv, rk_ctx, rv_ctx)
    att_o = _attention(q, k_lat, k_ctx, vt_lat, vt_ctx)

    wrt = jnp.pad(router_w[0], ((0, 0), (0, LANES - N_EXPERTS))).astype(BF16)
    brt = jnp.pad(router_b[0], (0, LANES - N_EXPERTS)).reshape(1, LANES)
    x1, h2, logits = _merge(x, ret_o, sgate, att_o, sgr, sgm, m_lat[2], m_lat[3], m_lat[4],
                            norm2_g[0].reshape(1, d), w_branch_ret[0].astype(BF16),
                            w_branch_mla[0].astype(BF16), w_out[0].astype(BF16), wrt, brt)

    n_tok = b * l
    h2_flat = h2.reshape(n_tok, d)
    gates, src_tok, row_of_flat, block_e, n_used = _route(logits.reshape(n_tok, LANES)[:, :N_EXPERTS], h2_flat)
    xs = h2_flat[src_tok]
    ys = _experts(block_e, n_used, xs, exp_w_gu[0].astype(BF16), exp_b_gu[0],
                  exp_w_down[0].astype(BF16), exp_b_down[0])
    yg = ys[row_of_flat].reshape(b, l, TOP_K * d)
    return _combine(x1, yg, gates.reshape(b, l, TOP_K), m_lat[5], final_norm_g.reshape(1, d))
```

```python
import functools

import numpy as np
import jax
import jax.numpy as jnp
from jax import lax
from jax.experimental import pallas as pl
from jax.experimental.pallas import tpu as pltpu

GRID_W = 64
N_RET_HEADS = 4
RET_QK_DIM = 256
RET_V_DIM = 512
N_MLA_HEADS = 8
MLA_Q_RANK = 384
MLA_KV_RANK = 256
MLA_QK_NOPE = 128
MLA_QK_ROPE = 64
MLA_V_DIM = 128
N_EXPERTS = 32
TOP_K = 4
SWIGLU_LIMIT = 7.0
SWIGLU_ALPHA = 1.702
ROPE_BASE = 10000.0
EPS = 1e-6

LANES = 128
MLA_HEAD_PAD = 2 * LANES
RET_CHUNK = 256
VMEM_LIMIT = 56 * 1024 * 1024

F32 = jnp.float32
BF16 = jnp.bfloat16


def _cparams(sem):
    return pltpu.CompilerParams(dimension_semantics=sem, vmem_limit_bytes=VMEM_LIMIT)


def _resident(shape):
    nd = len(shape)
    return pl.BlockSpec(shape, lambda *_: (0,) * nd, pipeline_mode=pl.Buffered(1))


def _dot(a, b):
    return jnp.dot(a, b, preferred_element_type=F32)


def _dot_nt(a, b):
    return lax.dot_general(a, b, (((1,), (1,)), ((), ())), preferred_element_type=F32)


def _dot_tn(a, b):
    return lax.dot_general(a, b, (((0,), (0,)), ((), ())), preferred_element_type=F32)


SUBLANES = 8


def _tile_rows_shape(n, d):
    assert d == SUBLANES * LANES
    return (n * SUBLANES, LANES)


def _store_tile_rows(ref, lead, val):
    rows = val.shape[0]
    for s in range(SUBLANES):
        ref[lead + (pl.ds(s, rows, stride=SUBLANES), slice(None))] = val[:, s * LANES:(s + 1) * LANES]


def _load_tile_rows(ref, lead, rows):
    return jnp.concatenate([ref[lead + (pl.ds(s, rows, stride=SUBLANES), slice(None))]
                            for s in range(SUBLANES)], axis=1)


def _tile_row(ref, r):
    return ref.at[pl.ds(pl.multiple_of(r * SUBLANES, SUBLANES), SUBLANES), :]


def _rms(x):
    return x * lax.rsqrt(jnp.mean(x * x, axis=-1, keepdims=True) + EPS)


def _sigmoid(x):
    return 1.0 / (1.0 + jnp.exp(-x))


def _ada_kernel(c_ref, w_ref, b_ref, o_ref):
    c = c_ref[...]
    s = c * _sigmoid(c)
    o_ref[...] = jnp.dot(s, w_ref[...], preferred_element_type=F32,
                         precision=lax.Precision.HIGHEST) + b_ref[...]


def _ada_mod(cvec, w, b):
    rows, d = cvec.shape
    n = w.shape[1]
    tn = 1024
    return pl.pallas_call(
        _ada_kernel,
        out_shape=jax.ShapeDtypeStruct((rows, n), F32),
        grid=(n // tn,),
        in_specs=[pl.BlockSpec((rows, d), lambda j: (0, 0)),
                  pl.BlockSpec((d, tn), lambda j: (0, j)),
                  pl.BlockSpec((1, tn), lambda j: (0, j))],
        out_specs=pl.BlockSpec((rows, tn), lambda j: (0, j)),
        compiler_params=_cparams(("arbitrary",)),
        name="ada_mod",
    )(cvec, w, b.reshape(1, n))


def _rope_tables(seq_len, qk_scale):
    pos = np.arange(seq_len)
    rows = (pos // GRID_W).astype(np.float32)
    cols = (pos % GRID_W).astype(np.float32)

    def angles(p, half):
        freqs = (np.float32(ROPE_BASE) ** (-np.arange(half, dtype=np.float32) / np.float32(half))).astype(np.float32)
        return (p[:, None] * freqs[None, :]).astype(np.float32)

    a_r, a_c = angles(rows, 64), angles(cols, 64)
    ret = np.concatenate([np.cos(a_r), np.cos(a_r), np.cos(a_c), np.cos(a_c),
                          -np.sin(a_r), np.sin(a_r), -np.sin(a_c), np.sin(a_c)], axis=1)
    b_r, b_c = angles(rows, 16), angles(cols, 16)
    z32 = np.zeros((seq_len, 32), np.float32)
    z64 = np.zeros((seq_len, 64), np.float32)
    cos = np.concatenate([np.cos(b_r), np.cos(b_c), np.cos(b_r), np.cos(b_c), z64], axis=1)
    sin_up = np.concatenate([-np.sin(b_r), -np.sin(b_c), z32, z64], axis=1)
    sin_dn = np.concatenate([z32, np.sin(b_r), np.sin(b_c), z64], axis=1)
    mla_k = np.concatenate([cos, sin_up, sin_dn], axis=1)
    mla = np.concatenate([mla_k, mla_k * np.float32(qk_scale)], axis=1)
    return jnp.asarray(ret, F32), jnp.asarray(mla, F32)


def _rope_ret(t, tab, parity):
    cos = tab[:, parity * LANES:(parity + 1) * LANES]
    sin = tab[:, (2 + parity) * LANES:(3 + parity) * LANES]
    return t * cos + pltpu.roll(t, 64, 1) * sin


def _rope_mla(t, tab, base):
    cos = tab[:, base:base + LANES]
    sin_up = tab[:, base + LANES:base + 2 * LANES]
    sin_dn = tab[:, base + 2 * LANES:base + 3 * LANES]
    return t * cos + pltpu.roll(t, 96, 1) * sin_up + pltpu.roll(t, 32, 1) * sin_dn


_C_Q, _C_K, _C_V, _C_G = 0, 1024, 2048, 4096
_C_CQ, _C_CKV, _C_KR, _C_GR, _C_GM, _C_END = 6144, 6528, 6784, 6912, 7936, 8960
_X_K, _X_V, _X_CKV, _X_KR, _X_END = 0, 1024, 3072, 3328, 3456


def _norm_mod(x, g, shift, scale):
    return (_rms(x) * g) * (1.0 + scale) + shift


def _mla_kv(ckv_acc, kr, gkv_ref, wuk_ref, wuvt_ref, k_ref, vt_ref):
    ckvn = (_rms(ckv_acc) * gkv_ref[...]).astype(BF16)
    kn = _dot(ckvn, wuk_ref[...])
    krb = kr.astype(BF16)
    for hh in range(N_MLA_HEADS):
        k_ref[0, :, hh * MLA_HEAD_PAD:hh * MLA_HEAD_PAD + LANES] = kn[:, hh * LANES:(hh + 1) * LANES].astype(BF16)
        k_ref[0, :, hh * MLA_HEAD_PAD + LANES:(hh + 1) * MLA_HEAD_PAD] = krb
    vt_ref[0] = _dot_nt(wuvt_ref[...], ckvn).astype(BF16)


def _inproj_lat_kernel(qk_scale, x_ref, shift_ref, scale_ref, g_ref, w_ref, wuq_ref, wuk_ref, wuvt_ref, gq_ref,
                       gkv_ref, tret_ref, tmla_ref,
                       rq_ref, rk_ref, rv_ref, sgate_ref, q_ref, k_ref, vt_ref, sgr_ref, sgm_ref):
    h = _norm_mod(x_ref[0], g_ref[...], shift_ref[0], scale_ref[0]).astype(BF16)
    tret = tret_ref[...]
    tmla = tmla_ref[...]

    def proj(c0, n):
        return _dot(h, w_ref[:, c0:c0 + n])

    for base, out in ((_C_Q, rq_ref), (_C_K, rk_ref)):
        for j in range(2):
            acc = proj(base + j * 512, 512)
            for blk in range(4):
                t = acc[:, blk * LANES:(blk + 1) * LANES]
                col = j * 512 + blk * LANES
                out[0, :, col:col + LANES] = _rope_ret(t, tret, blk % 2).astype(BF16)
    for j in range(4):
        rv_ref[0, :, j * 512:(j + 1) * 512] = proj(_C_V + j * 512, 512).astype(BF16)
    for j in range(4):
        a = proj(_C_G + j * 512, 512)
        sgate_ref[0, :, j * 512:(j + 1) * 512] = (a * _sigmoid(a)).astype(BF16)
    for j in range(2):
        a = proj(_C_GR + j * 512, 512)
        sgr_ref[0, :, j * 512:(j + 1) * 512] = _sigmoid(a).astype(BF16)
    for j in range(2):
        a = proj(_C_GM + j * 512, 512)
        sgm_ref[0, :, j * 512:(j + 1) * 512] = _sigmoid(a).astype(BF16)

    cqn = (_rms(proj(_C_CQ, MLA_Q_RANK)) * gq_ref[...]).astype(BF16)
    for j in range(4):
        acc = _dot(cqn, wuq_ref[:, j * 512:(j + 1) * 512])
        for blk in range(4):
            t = acc[:, blk * LANES:(blk + 1) * LANES]
            col = j * 512 + blk * LANES
            if blk % 2 == 0:
                q_ref[0, :, col:col + LANES] = (t * qk_scale).astype(BF16)
            else:
                q_ref[0, :, col:col + LANES] = _rope_mla(t, tmla, 3 * LANES).astype(BF16)
    kr = _rope_mla(proj(_C_KR, LANES), tmla, 0)
    _mla_kv(proj(_C_CKV, MLA_KV_RANK), kr, gkv_ref, wuk_ref, wuvt_ref, k_ref, vt_ref)


def _inproj_ctx_kernel(x_ref, shift_ref, scale_ref, g_ref, w_ref, wuk_ref, wuvt_ref, gkv_ref,
                       rk_ref, rv_ref, k_ref, vt_ref):
    h = _norm_mod(x_ref[0], g_ref[...], shift_ref[...], scale_ref[...]).astype(BF16)

    def proj(c0, n):
        return _dot(h, w_ref[:, c0:c0 + n])

    for j in range(2):
        rk_ref[0, :, j * 512:(j + 1) * 512] = proj(_X_K + j * 512, 512).astype(BF16)
    for j in range(4):
        rv_ref[0, :, j * 512:(j + 1) * 512] = proj(_X_V + j * 512, 512).astype(BF16)
    _mla_kv(proj(_X_CKV, MLA_KV_RANK), proj(_X_KR, LANES), gkv_ref, wuk_ref, wuvt_ref, k_ref, vt_ref)


def _prep_weights(w_in, w_uq, w_ukv):
    nq = N_RET_HEADS * RET_QK_DIM
    nv = N_RET_HEADS * RET_V_DIM
    d = w_in.shape[0]
    sizes = (nq, nq, nv, nv, MLA_Q_RANK, MLA_KV_RANK, MLA_QK_ROPE, d, d)
    offs = np.cumsum((0,) + sizes)
    wq, wk, wv, wg, wcq, wckv, wkr, wgr, wgm = [w_in[:, offs[i]:offs[i + 1]] for i in range(9)]
    wk = wk * (RET_QK_DIM ** -0.5)
    perm = np.concatenate([np.arange(0, 16), np.arange(32, 48), np.arange(16, 32), np.arange(48, 64)])
    wkr = jnp.pad(wkr[:, perm], ((0, 0), (0, LANES - MLA_QK_ROPE)))
    w_lat = jnp.concatenate([wq, wk, wv, wg, wcq, wckv, wkr, wgr, wgm], axis=1).astype(BF16)
    w_ctx = jnp.concatenate([wk, wv, wckv, wkr], axis=1).astype(BF16)
    uq = w_uq.reshape(MLA_Q_RANK, N_MLA_HEADS, MLA_QK_NOPE + MLA_QK_ROPE)
    uq = jnp.concatenate([uq[:, :, :MLA_QK_NOPE], uq[:, :, MLA_QK_NOPE:][:, :, perm],
                          jnp.zeros((MLA_Q_RANK, N_MLA_HEADS, LANES - MLA_QK_ROPE), w_uq.dtype)], axis=2)
    uq = uq.reshape(MLA_Q_RANK, N_MLA_HEADS * MLA_HEAD_PAD).astype(BF16)
    ukv = w_ukv.reshape(MLA_KV_RANK, N_MLA_HEADS, MLA_QK_NOPE + MLA_V_DIM)
    uk = ukv[:, :, :MLA_QK_NOPE].reshape(MLA_KV_RANK, -1).astype(BF16)
    uvt = ukv[:, :, MLA_QK_NOPE:].reshape(MLA_KV_RANK, -1).T.astype(BF16)
    return w_lat, w_ctx, uq, uk, uvt


def _tok_spec(tm, n):
    return pl.BlockSpec((1, tm, n), lambda i, j: (i, j, 0))


def _vt_spec(tm):
    return pl.BlockSpec((1, N_MLA_HEADS * MLA_V_DIM, tm), lambda i, j: (i, 0, j))


def _inproj_lat(x, shift, scale, g, w_lat, uq, uk, uvt, gq, gkv, tret, tmla, qk_scale):
    b, l, d = x.shape
    tm = 256
    vec = pl.BlockSpec((1, 1, d), lambda i, j: (i, 0, 0))
    out_w = (1024, 1024, 2048, 2048, 2048, 2048, None, 1024, 1024)
    nvt = N_MLA_HEADS * MLA_V_DIM
    return pl.pallas_call(
        functools.partial(_inproj_lat_kernel, qk_scale),
        out_shape=[jax.ShapeDtypeStruct((b, nvt, l) if n is None else (b, l, n), BF16) for n in out_w],
        grid=(b, l // tm),
        in_specs=[_tok_spec(tm, d), vec, vec, _resident((1, d)), _resident(w_lat.shape), _resident(uq.shape),
                  _resident(uk.shape), _resident(uvt.shape), _resident((1, MLA_Q_RANK)),
                  _resident((1, MLA_KV_RANK)),
                  pl.BlockSpec((tm, tret.shape[1]), lambda i, j: (j, 0)),
                  pl.BlockSpec((tm, tmla.shape[1]), lambda i, j: (j, 0))],
        out_specs=[_vt_spec(tm) if n is None else _tok_spec(tm, n) for n in out_w],
        compiler_params=_cparams(("arbitrary", "arbitrary")),
        name="inproj_lat",
    )(x, shift, scale, g, w_lat, uq, uk, uvt, gq, gkv, tret, tmla)


def _inproj_ctx(ctx, shift, scale, g, w_ctx, uk, uvt, gkv):
    b, l, d = ctx.shape
    tm = 256
    out_w = (1024, 2048, 2048, None)
    nvt = N_MLA_HEADS * MLA_V_DIM
    return pl.pallas_call(
        _inproj_ctx_kernel,
        out_shape=[jax.ShapeDtypeStruct((b, nvt, l) if n is None else (b, l, n), BF16) for n in out_w],
        grid=(b, l // tm),
        in_specs=[_tok_spec(tm, d), _resident((1, d)), _resident((1, d)), _resident((1, d)),
                  _resident(w_ctx.shape), _resident(uk.shape), _resident(uvt.shape),
                  _resident((1, MLA_KV_RANK))],
        out_specs=[_vt_spec(tm) if n is None else _tok_spec(tm, n) for n in out_w],
        compiler_params=_cparams(("arbitrary", "arbitrary")),
        name="inproj_ctx",
    )(ctx, shift, scale, g, w_ctx, uk, uvt, gkv)


def _retention_kernel(n_chunks, n_ctx_chunks, lg_ref, q_ref, k_ref, v_ref, kc_ref, vc_ref, o_ref,
                      ob_ref, sf_ref, sb_ref, dmat_ref, dec_ref):
    c = RET_CHUNK
    head = pl.program_id(1)
    lgf = lg_ref[0, head]
    lgb = lg_ref[1, head]
    ri = lax.broadcasted_iota(jnp.int32, (c, c), 0).astype(F32)
    ci = lax.broadcasted_iota(jnp.int32, (c, c), 1).astype(F32)
    diff = ri - ci
    dmat_ref[...] = jnp.where(diff >= 0, jnp.exp(lgf * jnp.maximum(diff, 0.0)),
                              jnp.exp(lgb * jnp.maximum(-diff, 0.0)))
    rk = lax.broadcasted_iota(jnp.int32, (c, RET_QK_DIM), 0).astype(F32)
    dec_ref[0] = jnp.exp(lgf * (rk + 1.0))
    dec_ref[1] = jnp.exp(lgf * (c - 1.0 - rk))
    dec_ref[2] = jnp.exp(lgb * (c - rk))
    dec_ref[3] = jnp.exp(lgb * rk)
    cdf = jnp.exp(jnp.full((1, RET_V_DIM), lgf * c, F32))
    cdb = jnp.exp(jnp.full((1, RET_V_DIM), lgb * c, F32))

    def scaled(t, which):
        return (t.astype(F32) * dec_ref[which]).astype(BF16)

    sf_ref[...] = jnp.zeros_like(sf_ref)
    sb_ref[...] = jnp.zeros_like(sb_ref)
    for n in range(n_ctx_chunks):
        sl = slice(n * c, (n + 1) * c)
        sf_ref[...] = sf_ref[...] * cdf + _dot_tn(scaled(kc_ref[0, sl, :], 1), vc_ref[0, sl, :])
    for n in reversed(range(n_ctx_chunks)):
        sl = slice(n * c, (n + 1) * c)
        sb_ref[...] = sb_ref[...] * cdb + _dot_tn(scaled(kc_ref[0, sl, :], 3), vc_ref[0, sl, :])

    def bwd_step(i, carry):
        r0 = pl.multiple_of((n_chunks - 1 - i) * c, c)
        q = q_ref[0, pl.ds(r0, c), :]
        k = k_ref[0, pl.ds(r0, c), :]
        v = v_ref[0, pl.ds(r0, c), :]
        ob_ref[pl.ds(r0, c), :] = _dot(scaled(q, 2), sb_ref[...].astype(BF16))
        sb_ref[...] = sb_ref[...] * cdb + _dot_tn(scaled(k, 3), v)
        return carry

    lax.fori_loop(0, n_chunks, bwd_step, 0)

    def fwd_step(i, carry):
        r0 = pl.multiple_of(i * c, c)
        q = q_ref[0, pl.ds(r0, c), :]
        k = k_ref[0, pl.ds(r0, c), :]
        v = v_ref[0, pl.ds(r0, c), :]
        s = (_dot_nt(q, k) * dmat_ref[...]).astype(BF16)
        o = _dot(s, v) + _dot(scaled(q, 0), sf_ref[...].astype(BF16)) + ob_ref[pl.ds(r0, c), :]
        o_ref[0, pl.ds(r0, c), :] = _rms(o).astype(BF16)
        sf_ref[...] = sf_ref[...] * cdf + _dot_tn(scaled(k, 1), v)
        return carry

    lax.fori_loop(0, n_chunks, fwd_step, 0)


def _retention(lg, rq, rk, rv, rk_ctx, rv_ctx):
    b, l, _ = rq.shape
    lc = rk_ctx.shape[1]
    c = RET_CHUNK
    assert l % c == 0 and lc % c == 0
    qk = lambda n: pl.BlockSpec((1, n, RET_QK_DIM), lambda i, h: (i, 0, h))
    vv = lambda n: pl.BlockSpec((1, n, RET_V_DIM), lambda i, h: (i, 0, h))
    return pl.pallas_call(
        functools.partial(_retention_kernel, l // c, lc // c),
        out_shape=jax.ShapeDtypeStruct((b, l, N_RET_HEADS * RET_V_DIM), BF16),
        grid=(b, N_RET_HEADS),
        in_specs=[pl.BlockSpec(memory_space=pltpu.SMEM), qk(l), qk(l), vv(l), qk(lc), vv(lc)],
        out_specs=vv(l),
        scratch_shapes=[pltpu.VMEM((l, RET_V_DIM), F32),
                        pltpu.VMEM((RET_QK_DIM, RET_V_DIM), F32),
                        pltpu.VMEM((RET_QK_DIM, RET_V_DIM), F32),
                        pltpu.VMEM((c, c), F32),
                        pltpu.VMEM((4, c, RET_QK_DIM), F32)],
        compiler_params=_cparams(("arbitrary", "arbitrary")),
        name="retention",
    )(lg, rq, rk, rv, rk_ctx, rv_ctx)


ATTN_CHAIN = 256
ATTN_CHAINS = 4


ATTN_KEY_BLOCK = 1024


def _attention_kernel(q_ref, kl_ref, kc_ref, vtl_ref, vtc_ref, o_ref):
    l = kl_ref.shape[1]
    lc = kc_ref.shape[1]
    blocks = [(kc_ref, vtc_ref, 0, lc)]
    blocks += [(kl_ref, vtl_ref, r0, ATTN_KEY_BLOCK) for r0 in range(0, l, ATTN_KEY_BLOCK)]
    chains = range(ATTN_CHAINS)
    rows = [slice(ch * ATTN_CHAIN, (ch + 1) * ATTN_CHAIN) for ch in chains]
    q = [q_ref[0, rows[ch], :] for ch in chains]

    def scores(ch, j):
        k_ref, _, r0, n = blocks[j]
        return _dot_nt(k_ref[0, r0:r0 + n, :], q[ch])

    s = [scores(ch, 0) for ch in chains]
    m = [None] * ATTN_CHAINS
    denom = [None] * ATTN_CHAINS
    acc = [None] * ATTN_CHAINS
    for j in range(len(blocks)):
        _, vt_ref, r0, n = blocks[j]
        for ch in chains:
            s_cur = s[ch]
            if j + 1 < len(blocks):
                s[ch] = scores(ch, j + 1)
            m_blk = jnp.max(s_cur, axis=0, keepdims=True)
            if j == 0:
                m[ch] = m_blk
                p = jnp.exp2(s_cur - m_blk)
                denom[ch] = jnp.sum(p, axis=0, keepdims=True)
                acc[ch] = _dot(vt_ref[0, :, r0:r0 + n], p.astype(BF16))
            else:
                m_new = jnp.maximum(m[ch], m_blk)
                alpha = jnp.exp2(m[ch] - m_new)
                p = jnp.exp2(s_cur - m_new)
                denom[ch] = denom[ch] * alpha + jnp.sum(p, axis=0, keepdims=True)
                acc[ch] = acc[ch] * alpha + _dot(vt_ref[0, :, r0:r0 + n], p.astype(BF16))
                m[ch] = m_new
    for ch in chains:
        o_ref[0, rows[ch], :] = (acc[ch] / denom[ch]).T.astype(BF16)


def _attention(q, k_lat, k_ctx, vt_lat, vt_ctx):
    b, l, _ = q.shape
    lc = k_ctx.shape[1]
    tq = ATTN_CHAIN * ATTN_CHAINS
    return pl.pallas_call(
        _attention_kernel,
        out_shape=jax.ShapeDtypeStruct((b, l, N_MLA_HEADS * MLA_V_DIM), BF16),
        grid=(b, N_MLA_HEADS, l // tq),
        in_specs=[pl.BlockSpec((1, tq, MLA_HEAD_PAD), lambda i, h, j: (i, j, h)),
                  pl.BlockSpec((1, l, MLA_HEAD_PAD), lambda i, h, j: (i, 0, h)),
                  pl.BlockSpec((1, lc, MLA_HEAD_PAD), lambda i, h, j: (i, 0, h)),
                  pl.BlockSpec((1, MLA_V_DIM, l), lambda i, h, j: (i, h, 0)),
                  pl.BlockSpec((1, MLA_V_DIM, lc), lambda i, h, j: (i, h, 0))],
        out_specs=pl.BlockSpec((1, tq, MLA_V_DIM), lambda i, h, j: (i, j, h)),
        compiler_params=_cparams(("arbitrary", "arbitrary", "arbitrary")),
        name="attention",
    )(q, k_lat, k_ctx, vt_lat, vt_ctx)


def _merge_kernel(x_ref, ret_ref, sgate_ref, att_ref, sgr_ref, sgm_ref, gate1_ref, shift2_ref, scale2_ref,
                  g2_ref, wr_ref, wm_ref, wo_ref, wrt_ref, brt_ref, x1_ref, h2_ref, logit_ref):
    r = (ret_ref[0].astype(F32) * sgate_ref[0].astype(F32)).astype(BF16)
    merged = (sgr_ref[0].astype(F32) * _dot(r, wr_ref[...])
              + sgm_ref[0].astype(F32) * _dot(att_ref[0], wm_ref[...]))
    y = _dot(merged.astype(BF16), wo_ref[...])
    x1 = x_ref[0] + gate1_ref[0] * y
    x1_ref[0] = x1
    h2 = _norm_mod(x1, g2_ref[...], shift2_ref[0], scale2_ref[0])
    _store_tile_rows(h2_ref, (0,), h2)
    logit_ref[0] = _dot(h2.astype(BF16), wrt_ref[...]) + brt_ref[...]


def _merge(x, ret_o, sgate, att_o, sgr, sgm, gate1, shift2, scale2, g2, wr, wm, wo, wrt, brt):
    b, l, d = x.shape
    tm = 256
    tok = lambda n: pl.BlockSpec((1, tm, n), lambda i, j: (i, j, 0))
    vec = pl.BlockSpec((1, 1, d), lambda i, j: (i, 0, 0))
    return pl.pallas_call(
        _merge_kernel,
        out_shape=[jax.ShapeDtypeStruct((b, l, d), F32),
                   jax.ShapeDtypeStruct((b,) + _tile_rows_shape(l, d), F32),
                   jax.ShapeDtypeStruct((b, l, LANES), F32)],
        grid=(b, l // tm),
        in_specs=[tok(d), tok(ret_o.shape[2]), tok(sgate.shape[2]), tok(att_o.shape[2]), tok(d), tok(d),
                  vec, vec, vec, _resident((1, d)), _resident(wr.shape), _resident(wm.shape),
                  _resident(wo.shape), _resident(wrt.shape), _resident(brt.shape)],
        out_specs=[tok(d), pl.BlockSpec((1,) + _tile_rows_shape(tm, d), lambda i, j: (i, j, 0)), tok(LANES)],
        compiler_params=_cparams(("arbitrary", "arbitrary")),
        name="merge",
    )(x, ret_o, sgate, att_o, sgr, sgm, gate1, shift2, scale2, g2, wr, wm, wo, wrt, brt)


EXPERT_BLOCK = 256


def _expert_kernel(be_ref, nb_ref, x_ref, wgu_ref, bgu_ref, wd_ref, bd_ref, y_ref):
    @pl.when(pl.program_id(0) >= nb_ref[0])
    def _():
        y_ref[...] = jnp.zeros_like(y_ref)

    @pl.when(pl.program_id(0) < nb_ref[0])
    def _():
        f = wd_ref.shape[1]
        x = _load_tile_rows(x_ref, (), EXPERT_BLOCK).astype(BF16)
        gate = _dot(x, wgu_ref[0, :, :f]) + bgu_ref[0, :, :f]
        up = _dot(x, wgu_ref[0, :, f:]) + bgu_ref[0, :, f:]
        gate = jnp.minimum(gate, SWIGLU_LIMIT)
        up = jnp.clip(up, -SWIGLU_LIMIT, SWIGLU_LIMIT)
        glu = gate * _sigmoid(SWIGLU_ALPHA * gate)
        act = ((up + 1.0) * glu).astype(BF16)
        _store_tile_rows(y_ref, (), _dot(act, wd_ref[0]) + bd_ref[0])


def _experts(block_e, n_used, xs, w_gu, b_gu, w_down, b_down):
    n_rows = xs.shape[0] // SUBLANES
    e, d, f2 = w_gu.shape
    f = f2 // 2
    bm = EXPERT_BLOCK
    blk = _tile_rows_shape(bm, d)
    return pl.pallas_call(
        _expert_kernel,
        out_shape=jax.ShapeDtypeStruct(xs.shape, F32),
        grid_spec=pltpu.PrefetchScalarGridSpec(
            num_scalar_prefetch=2,
            grid=(n_rows // bm,),
            in_specs=[pl.BlockSpec(blk, lambda i, be, nb: (jnp.minimum(i, nb[0] - 1), 0)),
                      pl.BlockSpec((1, d, f2), lambda i, be, nb: (be[i], 0, 0)),
                      pl.BlockSpec((1, 1, f2), lambda i, be, nb: (be[i], 0, 0)),
                      pl.BlockSpec((1, f, d), lambda i, be, nb: (be[i], 0, 0)),
                      pl.BlockSpec((1, 1, d), lambda i, be, nb: (be[i], 0, 0))],
            out_specs=pl.BlockSpec(blk, lambda i, be, nb: (i, 0))),
        compiler_params=_cparams(("arbitrary",)),
        name="experts",
    )(block_e, n_used, xs, w_gu, b_gu.reshape(e, 1, f2), w_down, b_down.reshape(e, 1, d))


ROUTE_TILE = 512
ROUTE_ROWS = 8
DMA_UNROLL = 8


def _router_kernel(logit_ref, idx_ref, gate_ref, rank_ref, cnt_ref, base_ref, tri_ref):
    tm = logit_ref.shape[0]
    step = pl.program_id(0)

    @pl.when(step == 0)
    def _():
        base_ref[...] = jnp.zeros_like(base_ref)
        r = lax.broadcasted_iota(jnp.int32, (tm, tm), 0)
        c = lax.broadcasted_iota(jnp.int32, (tm, tm), 1)
        tri_ref[...] = jnp.where(r < c, 1.0, 0.0).astype(BF16)

    v = logit_ref[...].T[:N_EXPERTS, :]
    eid = lax.broadcasted_iota(jnp.int32, (N_EXPERTS, tm), 0)
    onehot = jnp.zeros((N_EXPERTS, tm), F32)
    vals, sels = [], []
    for kk in range(TOP_K):
        mx = jnp.max(v, axis=0, keepdims=True)
        ik = jnp.min(jnp.where(v == mx, eid, N_EXPERTS), axis=0, keepdims=True)
        sel = eid == ik
        idx_ref[kk:kk + 1, :] = ik
        vals.append(mx)
        sels.append(sel)
        onehot = onehot + jnp.where(sel, 1.0, 0.0)
        v = jnp.where(sel, -jnp.inf, v)
    ex = [jnp.exp(val - vals[0]) for val in vals]
    tot = ex[0] + ex[1] + ex[2] + ex[3]
    for kk in range(TOP_K):
        gate_ref[kk:kk + 1, :] = ex[kk] / tot
    rank_e = base_ref[:, 0:1] + _dot(onehot.astype(BF16), tri_ref[...])
    for kk in range(TOP_K):
        rank_ref[kk:kk + 1, :] = jnp.sum(jnp.where(sels[kk], rank_e, 0.0), axis=0, keepdims=True).astype(jnp.int32)
    zero_i = jnp.zeros((ROUTE_ROWS - TOP_K, tm), jnp.int32)
    idx_ref[TOP_K:, :] = zero_i
    rank_ref[TOP_K:, :] = zero_i
    gate_ref[TOP_K:, :] = jnp.zeros((ROUTE_ROWS - TOP_K, tm), F32)
    base_ref[...] = base_ref[...] + jnp.sum(onehot, axis=1, keepdims=True)
    cnt_ref[...] = base_ref[...]


def _router(logits):
    n_tok = logits.shape[0]
    tm = ROUTE_TILE
    tab = lambda: pl.BlockSpec((ROUTE_ROWS, tm), lambda i: (0, i))
    return pl.pallas_call(
        _router_kernel,
        out_shape=[jax.ShapeDtypeStruct((ROUTE_ROWS, n_tok), jnp.int32),
                   jax.ShapeDtypeStruct((ROUTE_ROWS, n_tok), F32),
                   jax.ShapeDtypeStruct((ROUTE_ROWS, n_tok), jnp.int32),
                   jax.ShapeDtypeStruct((N_EXPERTS, LANES), F32)],
        grid=(n_tok // tm,),
        in_specs=[pl.BlockSpec((tm, LANES), lambda i: (i, 0))],
        out_specs=[tab(), tab(), tab(), pl.BlockSpec((N_EXPERTS, LANES), lambda i: (0, 0))],
        scratch_shapes=[pltpu.VMEM((N_EXPERTS, LANES), F32), pltpu.VMEM((tm, tm), BF16)],
        compiler_params=_cparams(("arbitrary",)),
        name="router",
    )(logits)


def _layout(idx, rank, counts, n_tok):
    bm = EXPERT_BLOCK
    counts = counts[:, 0].astype(jnp.int32)
    padded = (counts + bm - 1) // bm * bm
    pad_end = jnp.cumsum(padded)
    pad_start = pad_end - padded
    n_rows = n_tok * TOP_K + N_EXPERTS * bm
    n_blocks = n_rows // bm
    blk_start = jnp.arange(n_blocks, dtype=jnp.int32) * bm
    block_e = jnp.minimum(jnp.sum(blk_start[:, None] >= pad_end[None, :], axis=1), N_EXPERTS - 1).astype(jnp.int32)
    n_used = (pad_end[-1] // bm).astype(jnp.int32).reshape(1)
    start_of = jnp.zeros_like(idx)
    for e in range(N_EXPERTS):
        start_of = jnp.where(idx == e, pad_start[e], start_of)
    dest = start_of + rank
    fill = jnp.stack([pad_start + counts, pad_end]).astype(jnp.int32)
    return dest, block_e, n_used, fill, n_rows


def _dispatch_kernel(fill_ref, dest_ref, h_ref, xs_ref, zero_ref, sem, zsem):
    tm = h_ref.shape[0] // SUBLANES

    def row_copy(t, kk):
        return pltpu.make_async_copy(_tile_row(h_ref, t), _tile_row(xs_ref, dest_ref[0, 0, t * TOP_K + kk]), sem)

    def issue(t, carry):
        for kk in range(TOP_K):
            row_copy(t, kk).start()
        return carry

    lax.fori_loop(0, tm, issue, 0, unroll=DMA_UNROLL)

    def drain(t, carry):
        for kk in range(TOP_K):
            row_copy(t, kk).wait()
        return carry

    lax.fori_loop(0, tm, drain, 0, unroll=DMA_UNROLL)

    @pl.when(pl.program_id(0) == pl.num_programs(0) - 1)
    def _():
        zero_ref[...] = jnp.zeros_like(zero_ref)

        def zero_copy(r):
            return pltpu.make_async_copy(zero_ref, _tile_row(xs_ref, r), zsem)

        def per_expert(e, carry):
            lo = fill_ref[0, e]
            hi = fill_ref[1, e]
            lax.fori_loop(lo, hi, lambda r, c: (zero_copy(r).start(), c)[1], 0)
            lax.fori_loop(lo, hi, lambda r, c: (zero_copy(r).wait(), c)[1], 0)
            return carry

        lax.fori_loop(0, N_EXPERTS, per_expert, 0)


def _dest_table(dest, tm):
    n_tok = dest.shape[1]
    return dest[:TOP_K].T.reshape(n_tok // tm, 1, tm * TOP_K)


def _dispatch(fill, dest, h2, n_rows):
    n_tok = h2.shape[0] // SUBLANES
    tm = ROUTE_TILE
    return pl.pallas_call(
        _dispatch_kernel,
        out_shape=jax.ShapeDtypeStruct((n_rows * SUBLANES, LANES), h2.dtype),
        grid_spec=pltpu.PrefetchScalarGridSpec(
            num_scalar_prefetch=1,
            grid=(n_tok // tm,),
            in_specs=[pl.BlockSpec((1, 1, tm * TOP_K), lambda i, f: (i, 0, 0), memory_space=pltpu.SMEM),
                      pl.BlockSpec((tm * SUBLANES, LANES), lambda i, f: (i, 0))],
            out_specs=pl.BlockSpec(memory_space=pl.ANY),
            scratch_shapes=[pltpu.VMEM((SUBLANES, LANES), h2.dtype), pltpu.SemaphoreType.DMA(()),
                            pltpu.SemaphoreType.DMA(())]),
        compiler_params=_cparams(("arbitrary",)),
        name="dispatch",
    )(fill, _dest_table(dest, tm), h2)


COMBINE_TILE = 256


def _combine_kernel(dest_ref, x1_ref, gates_ref, gate2_ref, gf_ref, ys_ref, o_ref, buf_ref, sem):
    tm = x1_ref.shape[1]

    def row_copy(t, kk):
        return pltpu.make_async_copy(_tile_row(ys_ref, dest_ref[0, 0, t * TOP_K + kk]),
                                     _tile_row(buf_ref.at[kk], t), sem)

    def issue(t, carry):
        for kk in range(TOP_K):
            row_copy(t, kk).start()
        return carry

    lax.fori_loop(0, tm, issue, 0, unroll=DMA_UNROLL)

    def drain(t, carry):
        for kk in range(TOP_K):
            row_copy(t, kk).wait()
        return carry

    lax.fori_loop(0, tm, drain, 0, unroll=DMA_UNROLL)

    gates = gates_ref[...]
    moe = gates[:, 0:1] * _load_tile_rows(buf_ref, (0,), tm)
    for kk in range(1, TOP_K):
        moe = moe + gates[:, kk:kk + 1] * _load_tile_rows(buf_ref, (kk,), tm)
    x2 = x1_ref[0] + gate2_ref[0] * moe
    o_ref[0] = _rms(x2) * gf_ref[...]


def _combine(dest, x1, gates_t, gate2, gf, ys):
    b, l, d = x1.shape
    tm = COMBINE_TILE
    per_b = l // tm
    return pl.pallas_call(
        _combine_kernel,
        out_shape=jax.ShapeDtypeStruct((b, l, d), F32),
        grid=(b, per_b),
        in_specs=[pl.BlockSpec((1, 1, tm * TOP_K), lambda i, j: (i * per_b + j, 0, 0), memory_space=pltpu.SMEM),
                  pl.BlockSpec((1, tm, d), lambda i, j: (i, j, 0)),
                  pl.BlockSpec((tm, TOP_K), lambda i, j: (i * per_b + j, 0)),
                  pl.BlockSpec((1, 1, d), lambda i, j: (i, 0, 0)),
                  _resident((1, d)),
                  pl.BlockSpec(memory_space=pl.ANY)],
        out_specs=pl.BlockSpec((1, tm, d), lambda i, j: (i, j, 0)),
        scratch_shapes=[pltpu.VMEM((TOP_K,) + _tile_rows_shape(tm, d), ys.dtype), pltpu.SemaphoreType.DMA(())],
        compiler_params=_cparams(("arbitrary", "arbitrary")),
        name="combine",
    )(_dest_table(dest, tm), x1, gates_t, gate2, gf, ys)


def kernel(x, c, ctx, c_ctx, norm1_g, norm2_g, ada_w, ada_b, w_in, ret_decay_fwd, ret_decay_bwd, mla_q_norm_g, mla_w_uq, mla_kv_norm_g, mla_w_ukv, w_branch_ret, w_branch_mla, w_out, router_w, router_b, exp_w_gu, exp_b_gu, exp_w_down, exp_b_down, final_norm_g):
    depth = norm1_g.shape[0]
    assert depth == 1, "single-layer block"
    b, l, d = x.shape
    qk_scale = float((MLA_QK_NOPE + MLA_QK_ROPE) ** -0.5 * np.log2(np.e))

    n_mod = b + 1
    rows = -(-n_mod // 8) * 8
    cvec = jnp.concatenate([c, c_ctx[None, :], jnp.zeros((rows - n_mod, d), F32)], axis=0)
    mod = _ada_mod(cvec, ada_w[0], ada_b[0])
    m_lat = [mod[:b, i * d:(i + 1) * d].reshape(b, 1, d) for i in range(6)]
    m_ctx = [mod[b:b + 1, i * d:(i + 1) * d] for i in range(2)]

    w_lat, w_ctx, uq, uk, uvt = _prep_weights(w_in[0], mla_w_uq[0], mla_w_ukv[0])
    tret, tmla = _rope_tables(l, qk_scale)
    g1 = norm1_g[0].reshape(1, d)
    gq = mla_q_norm_g[0].reshape(1, MLA_Q_RANK)
    gkv = mla_kv_norm_g[0].reshape(1, MLA_KV_RANK)

    rq, rk, rv, sgate, q, k_lat, vt_lat, sgr, sgm = _inproj_lat(
        x, m_lat[0], m_lat[1], g1, w_lat, uq, uk, uvt, gq, gkv, tret, tmla, qk_scale)
    rk_ctx, rv_ctx, k_ctx, vt_ctx = _inproj_ctx(ctx, m_ctx[0], m_ctx[1], g1, w_ctx, uk, uvt, gkv)

    lg = jnp.stack([jax.nn.log_sigmoid(ret_decay_fwd[0].astype(F32)),
                    jax.nn.log_sigmoid(ret_decay_bwd[0].astype(F32))])
    ret_o = _retention(lg, rq, rk, rv, rk_ctx, rv_ctx)
    att_o = _attention(q, k_lat, k_ctx, vt_lat, vt_ctx)

    wrt = jnp.pad(router_w[0], ((0, 0), (0, LANES - N_EXPERTS))).astype(BF16)
    brt = jnp.pad(router_b[0], (0, LANES - N_EXPERTS)).reshape(1, LANES)
    x1, h2, logits = _merge(x, ret_o, sgate, att_o, sgr, sgm, m_lat[2], m_lat[3], m_lat[4],
                            norm2_g[0].reshape(1, d), w_branch_ret[0].astype(BF16),
                            w_branch_mla[0].astype(BF16), w_out[0].astype(BF16), wrt, brt)

    n_tok = b * l
    idx, gates, rank, counts = _router(logits.reshape(n_tok, LANES))
    dest, block_e, n_used, fill, n_rows = _layout(idx, rank, counts, n_tok)
    xs = _dispatch(fill, dest, h2.reshape(_tile_rows_shape(n_tok, d)), n_rows)
    ys = _experts(block_e, n_used, xs, exp_w_gu[0].astype(BF16), exp_b_gu[0],
                  exp_w_down[0].astype(BF16), exp_b_down[0])
    return _combine(dest, x1, gates[:TOP_K].T, m_lat[5], final_norm_g.reshape(1, d), ys)
```

```python
import functools

import numpy as np
import jax
import jax.numpy as jnp
from jax import lax
from jax.experimental import pallas as pl
from jax.experimental.pallas import tpu as pltpu

GRID_W = 64
N_RET_HEADS = 4
RET_QK_DIM = 256
RET_V_DIM = 512
N_MLA_HEADS = 8
MLA_Q_RANK = 384
MLA_KV_RANK = 256
MLA_QK_NOPE = 128
MLA_QK_ROPE = 64
MLA_V_DIM = 128
N_EXPERTS = 32
TOP_K = 4
SWIGLU_LIMIT = 7.0
SWIGLU_ALPHA = 1.702
ROPE_BASE = 10000.0
EPS = 1e-6

LANES = 128
MLA_HEAD_PAD = 2 * LANES
RET_CHUNK = 256
VMEM_LIMIT = 56 * 1024 * 1024
INPROJ_TILE = 256
MERGE_TILE = 256

F32 = jnp.float32
BF16 = jnp.bfloat16


def _cparams(sem):
    return pltpu.CompilerParams(dimension_semantics=sem, vmem_limit_bytes=VMEM_LIMIT)


def _resident(shape):
    nd = len(shape)
    return pl.BlockSpec(shape, lambda *_: (0,) * nd, pipeline_mode=pl.Buffered(1))


def _dot(a, b):
    return jnp.dot(a, b, preferred_element_type=F32)


def _dot_nt(a, b):
    return lax.dot_general(a, b, (((1,), (1,)), ((), ())), preferred_element_type=F32)


def _dot_tn(a, b):
    return lax.dot_general(a, b, (((0,), (0,)), ((), ())), preferred_element_type=F32)


SUBLANES = 8


def _tile_rows_shape(n, d):
    assert d == SUBLANES * LANES
    return (n * SUBLANES, LANES)


def _store_tile_rows(ref, lead, val):
    rows = val.shape[0]
    for s in range(SUBLANES):
        ref[lead + (pl.ds(s, rows, stride=SUBLANES), slice(None))] = val[:, s * LANES:(s + 1) * LANES]


def _load_tile_rows(ref, lead, rows):
    return jnp.concatenate([ref[lead + (pl.ds(s, rows, stride=SUBLANES), slice(None))]
                            for s in range(SUBLANES)], axis=1)


def _tile_row(ref, r):
    return ref.at[pl.ds(pl.multiple_of(r * SUBLANES, SUBLANES), SUBLANES), :]


def _rms(x):
    return x * lax.rsqrt(jnp.mean(x * x, axis=-1, keepdims=True) + EPS)


def _sigmoid(x):
    return 1.0 / (1.0 + jnp.exp(-x))


def _ada_kernel(c_ref, w_ref, b_ref, o_ref):
    c = c_ref[...]
    s = c * _sigmoid(c)
    o_ref[...] = jnp.dot(s, w_ref[...], preferred_element_type=F32,
                         precision=lax.Precision.HIGHEST) + b_ref[...]


def _ada_mod(cvec, w, b):
    rows, d = cvec.shape
    n = w.shape[1]
    tn = 1024
    return pl.pallas_call(
        _ada_kernel,
        out_shape=jax.ShapeDtypeStruct((rows, n), F32),
        grid=(n // tn,),
        in_specs=[pl.BlockSpec((rows, d), lambda j: (0, 0)),
                  pl.BlockSpec((d, tn), lambda j: (0, j)),
                  pl.BlockSpec((1, tn), lambda j: (0, j))],
        out_specs=pl.BlockSpec((rows, tn), lambda j: (0, j)),
        compiler_params=_cparams(("arbitrary",)),
        name="ada_mod",
    )(cvec, w, b.reshape(1, n))


def _rope_tables(seq_len, qk_scale):
    pos = np.arange(seq_len)
    rows = (pos // GRID_W).astype(np.float32)
    cols = (pos % GRID_W).astype(np.float32)

    def angles(p, half):
        freqs = (np.float32(ROPE_BASE) ** (-np.arange(half, dtype=np.float32) / np.float32(half))).astype(np.float32)
        return (p[:, None] * freqs[None, :]).astype(np.float32)

    a_r, a_c = angles(rows, 64), angles(cols, 64)
    ret = np.concatenate([np.cos(a_r), np.cos(a_r), np.cos(a_c), np.cos(a_c),
                          -np.sin(a_r), np.sin(a_r), -np.sin(a_c), np.sin(a_c)], axis=1)
    b_r, b_c = angles(rows, 16), angles(cols, 16)
    z32 = np.zeros((seq_len, 32), np.float32)
    z64 = np.zeros((seq_len, 64), np.float32)
    cos = np.concatenate([np.cos(b_r), np.cos(b_c), np.cos(b_r), np.cos(b_c), z64], axis=1)
    sin_up = np.concatenate([-np.sin(b_r), -np.sin(b_c), z32, z64], axis=1)
    sin_dn = np.concatenate([z32, np.sin(b_r), np.sin(b_c), z64], axis=1)
    mla_k = np.concatenate([cos, sin_up, sin_dn], axis=1)
    mla = np.concatenate([mla_k, mla_k * np.float32(qk_scale)], axis=1)
    return jnp.asarray(ret, F32), jnp.asarray(mla, F32)


def _rope_ret(t, tab, parity):
    cos = tab[:, parity * LANES:(parity + 1) * LANES]
    sin = tab[:, (2 + parity) * LANES:(3 + parity) * LANES]
    return t * cos + pltpu.roll(t, 64, 1) * sin


def _rope_mla(t, tab, base):
    cos = tab[:, base:base + LANES]
    sin_up = tab[:, base + LANES:base + 2 * LANES]
    sin_dn = tab[:, base + 2 * LANES:base + 3 * LANES]
    return t * cos + pltpu.roll(t, 96, 1) * sin_up + pltpu.roll(t, 32, 1) * sin_dn


_C_Q, _C_K, _C_V, _C_G = 0, 1024, 2048, 4096
_C_CQ, _C_CKV, _C_KR, _C_GR, _C_GM, _C_END = 6144, 6528, 6784, 6912, 7936, 8960
_X_K, _X_V, _X_CKV, _X_KR, _X_END = 0, 1024, 3072, 3328, 3456


def _norm_mod(x, g, shift, scale):
    return (_rms(x) * g) * (1.0 + scale) + shift


def _mla_kv(ckv_acc, kr, gkv_ref, wuk_ref, wuvt_ref, k_ref, vt_ref):
    ckvn = (_rms(ckv_acc) * gkv_ref[...]).astype(BF16)
    kn = _dot(ckvn, wuk_ref[...])
    krb = kr.astype(BF16)
    for hh in range(N_MLA_HEADS):
        k_ref[0, :, hh * MLA_HEAD_PAD:hh * MLA_HEAD_PAD + LANES] = kn[:, hh * LANES:(hh + 1) * LANES].astype(BF16)
        k_ref[0, :, hh * MLA_HEAD_PAD + LANES:(hh + 1) * MLA_HEAD_PAD] = krb
    vt_ref[0] = _dot_nt(wuvt_ref[...], ckvn).astype(BF16)


def _inproj_lat_kernel(qk_scale, x_ref, shift_ref, scale_ref, g_ref, w_ref, wuq_ref, wuk_ref, wuvt_ref, gq_ref,
                       gkv_ref, tret_ref, tmla_ref,
                       rq_ref, rk_ref, rv_ref, sgate_ref, q_ref, k_ref, vt_ref, sgr_ref, sgm_ref):
    h = _norm_mod(x_ref[0], g_ref[...], shift_ref[0], scale_ref[0]).astype(BF16)
    tret = tret_ref[...]
    tmla = tmla_ref[...]

    def proj(c0, n):
        return _dot(h, w_ref[:, c0:c0 + n])

    for base, out in ((_C_Q, rq_ref), (_C_K, rk_ref)):
        for j in range(2):
            acc = proj(base + j * 512, 512)
            for blk in range(4):
                t = acc[:, blk * LANES:(blk + 1) * LANES]
                col = j * 512 + blk * LANES
                out[0, :, col:col + LANES] = _rope_ret(t, tret, blk % 2).astype(BF16)
    for j in range(4):
        rv_ref[0, :, j * 512:(j + 1) * 512] = proj(_C_V + j * 512, 512).astype(BF16)
    for j in range(4):
        a = proj(_C_G + j * 512, 512)
        sgate_ref[0, :, j * 512:(j + 1) * 512] = (a * _sigmoid(a)).astype(BF16)
    for j in range(2):
        a = proj(_C_GR + j * 512, 512)
        sgr_ref[0, :, j * 512:(j + 1) * 512] = _sigmoid(a).astype(BF16)
    for j in range(2):
        a = proj(_C_GM + j * 512, 512)
        sgm_ref[0, :, j * 512:(j + 1) * 512] = _sigmoid(a).astype(BF16)

    cqn = (_rms(proj(_C_CQ, MLA_Q_RANK)) * gq_ref[...]).astype(BF16)
    for j in range(4):
        acc = _dot(cqn, wuq_ref[:, j * 512:(j + 1) * 512])
        for blk in range(4):
            t = acc[:, blk * LANES:(blk + 1) * LANES]
            col = j * 512 + blk * LANES
            if blk % 2 == 0:
                q_ref[0, :, col:col + LANES] = (t * qk_scale).astype(BF16)
            else:
                q_ref[0, :, col:col + LANES] = _rope_mla(t, tmla, 3 * LANES).astype(BF16)
    kr = _rope_mla(proj(_C_KR, LANES), tmla, 0)
    _mla_kv(proj(_C_CKV, MLA_KV_RANK), kr, gkv_ref, wuk_ref, wuvt_ref, k_ref, vt_ref)


def _inproj_ctx_kernel(x_ref, shift_ref, scale_ref, g_ref, w_ref, wuk_ref, wuvt_ref, gkv_ref,
                       rk_ref, rv_ref, k_ref, vt_ref):
    h = _norm_mod(x_ref[0], g_ref[...], shift_ref[...], scale_ref[...]).astype(BF16)

    def proj(c0, n):
        return _dot(h, w_ref[:, c0:c0 + n])

    for j in range(2):
        rk_ref[0, :, j * 512:(j + 1) * 512] = proj(_X_K + j * 512, 512).astype(BF16)
    for j in range(4):
        rv_ref[0, :, j * 512:(j + 1) * 512] = proj(_X_V + j * 512, 512).astype(BF16)
    _mla_kv(proj(_X_CKV, MLA_KV_RANK), proj(_X_KR, LANES), gkv_ref, wuk_ref, wuvt_ref, k_ref, vt_ref)


def _prep_weights(w_in, w_uq, w_ukv):
    nq = N_RET_HEADS * RET_QK_DIM
    nv = N_RET_HEADS * RET_V_DIM
    d = w_in.shape[0]
    sizes = (nq, nq, nv, nv, MLA_Q_RANK, MLA_KV_RANK, MLA_QK_ROPE, d, d)
    offs = np.cumsum((0,) + sizes)
    wq, wk, wv, wg, wcq, wckv, wkr, wgr, wgm = [w_in[:, offs[i]:offs[i + 1]] for i in range(9)]
    wk = wk * (RET_QK_DIM ** -0.5)
    perm = np.concatenate([np.arange(0, 16), np.arange(32, 48), np.arange(16, 32), np.arange(48, 64)])
    wkr = jnp.pad(wkr[:, perm], ((0, 0), (0, LANES - MLA_QK_ROPE)))
    w_lat = jnp.concatenate([wq, wk, wv, wg, wcq, wckv, wkr, wgr, wgm], axis=1).astype(BF16)
    w_ctx = jnp.concatenate([wk, wv, wckv, wkr], axis=1).astype(BF16)
    uq = w_uq.reshape(MLA_Q_RANK, N_MLA_HEADS, MLA_QK_NOPE + MLA_QK_ROPE)
    uq = jnp.concatenate([uq[:, :, :MLA_QK_NOPE], uq[:, :, MLA_QK_NOPE:][:, :, perm],
                          jnp.zeros((MLA_Q_RANK, N_MLA_HEADS, LANES - MLA_QK_ROPE), w_uq.dtype)], axis=2)
    uq = uq.reshape(MLA_Q_RANK, N_MLA_HEADS * MLA_HEAD_PAD).astype(BF16)
    ukv = w_ukv.reshape(MLA_KV_RANK, N_MLA_HEADS, MLA_QK_NOPE + MLA_V_DIM)
    uk = ukv[:, :, :MLA_QK_NOPE].reshape(MLA_KV_RANK, -1).astype(BF16)
    uvt = ukv[:, :, MLA_QK_NOPE:].reshape(MLA_KV_RANK, -1).T.astype(BF16)
    return w_lat, w_ctx, uq, uk, uvt


def _tok_spec(tm, n):
    return pl.BlockSpec((1, tm, n), lambda i, j: (i, j, 0))


def _vt_spec(tm):
    return pl.BlockSpec((1, N_MLA_HEADS * MLA_V_DIM, tm), lambda i, j: (i, 0, j))


def _inproj_lat(x, shift, scale, g, w_lat, uq, uk, uvt, gq, gkv, tret, tmla, qk_scale):
    b, l, d = x.shape
    tm = INPROJ_TILE
    vec = pl.BlockSpec((1, 1, d), lambda i, j: (i, 0, 0))
    out_w = (1024, 1024, 2048, 2048, 2048, 2048, None, 1024, 1024)
    nvt = N_MLA_HEADS * MLA_V_DIM
    return pl.pallas_call(
        functools.partial(_inproj_lat_kernel, qk_scale),
        out_shape=[jax.ShapeDtypeStruct((b, nvt, l) if n is None else (b, l, n), BF16) for n in out_w],
        grid=(b, l // tm),
        in_specs=[_tok_spec(tm, d), vec, vec, _resident((1, d)), _resident(w_lat.shape), _resident(uq.shape),
                  _resident(uk.shape), _resident(uvt.shape), _resident((1, MLA_Q_RANK)),
                  _resident((1, MLA_KV_RANK)),
                  pl.BlockSpec((tm, tret.shape[1]), lambda i, j: (j, 0)),
                  pl.BlockSpec((tm, tmla.shape[1]), lambda i, j: (j, 0))],
        out_specs=[_vt_spec(tm) if n is None else _tok_spec(tm, n) for n in out_w],
        compiler_params=_cparams(("arbitrary", "arbitrary")),
        name="inproj_lat",
    )(x, shift, scale, g, w_lat, uq, uk, uvt, gq, gkv, tret, tmla)


def _inproj_ctx(ctx, shift, scale, g, w_ctx, uk, uvt, gkv):
    b, l, d = ctx.shape
    tm = 256
    out_w = (1024, 2048, 2048, None)
    nvt = N_MLA_HEADS * MLA_V_DIM
    return pl.pallas_call(
        _inproj_ctx_kernel,
        out_shape=[jax.ShapeDtypeStruct((b, nvt, l) if n is None else (b, l, n), BF16) for n in out_w],
        grid=(b, l // tm),
        in_specs=[_tok_spec(tm, d), _resident((1, d)), _resident((1, d)), _resident((1, d)),
                  _resident(w_ctx.shape), _resident(uk.shape), _resident(uvt.shape),
                  _resident((1, MLA_KV_RANK))],
        out_specs=[_vt_spec(tm) if n is None else _tok_spec(tm, n) for n in out_w],
        compiler_params=_cparams(("arbitrary", "arbitrary")),
        name="inproj_ctx",
    )(ctx, shift, scale, g, w_ctx, uk, uvt, gkv)


def _retention_kernel(n_chunks, n_ctx_chunks, lg_ref, q_ref, k_ref, v_ref, kc_ref, vc_ref, o_ref,
                      ob_ref, sf_ref, sb_ref, dmat_ref, dec_ref):
    c = RET_CHUNK
    head = pl.program_id(1)
    lgf = lg_ref[0, head]
    lgb = lg_ref[1, head]
    ri = lax.broadcasted_iota(jnp.int32, (c, c), 0).astype(F32)
    ci = lax.broadcasted_iota(jnp.int32, (c, c), 1).astype(F32)
    diff = ri - ci
    dmat_ref[...] = jnp.where(diff >= 0, jnp.exp(lgf * jnp.maximum(diff, 0.0)),
                              jnp.exp(lgb * jnp.maximum(-diff, 0.0)))
    rk = lax.broadcasted_iota(jnp.int32, (c, RET_QK_DIM), 0).astype(F32)
    dec_ref[0] = jnp.exp(lgf * (rk + 1.0))
    dec_ref[1] = jnp.exp(lgf * (c - 1.0 - rk))
    dec_ref[2] = jnp.exp(lgb * (c - rk))
    dec_ref[3] = jnp.exp(lgb * rk)
    cdf = jnp.exp(jnp.full((1, RET_V_DIM), lgf * c, F32))
    cdb = jnp.exp(jnp.full((1, RET_V_DIM), lgb * c, F32))

    def scaled(t, which):
        return (t.astype(F32) * dec_ref[which]).astype(BF16)

    sf_ref[...] = jnp.zeros_like(sf_ref)
    sb_ref[...] = jnp.zeros_like(sb_ref)
    for n in range(n_ctx_chunks):
        sl = slice(n * c, (n + 1) * c)
        sf_ref[...] = sf_ref[...] * cdf + _dot_tn(scaled(kc_ref[0, sl, :], 1), vc_ref[0, sl, :])
    for n in reversed(range(n_ctx_chunks)):
        sl = slice(n * c, (n + 1) * c)
        sb_ref[...] = sb_ref[...] * cdb + _dot_tn(scaled(kc_ref[0, sl, :], 3), vc_ref[0, sl, :])

    def step(i, finish):
        rb = pl.multiple_of((n_chunks - 1 - i) * c, c)
        rf = pl.multiple_of(i * c, c)
        qf = q_ref[0, pl.ds(rf, c), :]
        kf = k_ref[0, pl.ds(rf, c), :]
        vf = v_ref[0, pl.ds(rf, c), :]
        qb = q_ref[0, pl.ds(rb, c), :]
        kb = k_ref[0, pl.ds(rb, c), :]
        vb = v_ref[0, pl.ds(rb, c), :]
        s = _dot_nt(qf, kf)
        o_b = _dot(scaled(qb, 2), sb_ref[...].astype(BF16))
        sb_new = _dot_tn(scaled(kb, 3), vb)
        o_f = _dot(scaled(qf, 0), sf_ref[...].astype(BF16))
        sf_new = _dot_tn(scaled(kf, 1), vf)
        o_f = o_f + _dot((s * dmat_ref[...]).astype(BF16), vf)
        sb_ref[...] = sb_ref[...] * cdb + sb_new
        sf_ref[...] = sf_ref[...] * cdf + sf_new
        if finish:
            o_ref[0, pl.ds(rb, c), :] = _rms(o_b + ob_ref[pl.ds(rb, c), :]).astype(BF16)
            o_ref[0, pl.ds(rf, c), :] = _rms(o_f + ob_ref[pl.ds(rf, c), :]).astype(BF16)
        else:
            ob_ref[pl.ds(rb, c), :] = o_b
            ob_ref[pl.ds(rf, c), :] = o_f

    half = n_chunks // 2
    lax.fori_loop(0, half, lambda i, carry: (step(i, False), carry)[1], 0)
    lax.fori_loop(half, n_chunks, lambda i, carry: (step(i, True), carry)[1], 0)


def _retention(lg, rq, rk, rv, rk_ctx, rv_ctx):
    b, l, _ = rq.shape
    lc = rk_ctx.shape[1]
    c = RET_CHUNK
    assert l % (2 * c) == 0 and lc % c == 0
    qk = lambda n: pl.BlockSpec((1, n, RET_QK_DIM), lambda i, h: (i, 0, h))
    vv = lambda n: pl.BlockSpec((1, n, RET_V_DIM), lambda i, h: (i, 0, h))
    return pl.pallas_call(
        functools.partial(_retention_kernel, l // c, lc // c),
        out_shape=jax.ShapeDtypeStruct((b, l, N_RET_HEADS * RET_V_DIM), BF16),
        grid=(b, N_RET_HEADS),
        in_specs=[pl.BlockSpec(memory_space=pltpu.SMEM), qk(l), qk(l), vv(l), qk(lc), vv(lc)],
        out_specs=vv(l),
        scratch_shapes=[pltpu.VMEM((l, RET_V_DIM), F32),
                        pltpu.VMEM((RET_QK_DIM, RET_V_DIM), F32),
                        pltpu.VMEM((RET_QK_DIM, RET_V_DIM), F32),
                        pltpu.VMEM((c, c), F32),
                        pltpu.VMEM((4, c, RET_QK_DIM), F32)],
        compiler_params=_cparams(("arbitrary", "arbitrary")),
        name="retention",
    )(lg, rq, rk, rv, rk_ctx, rv_ctx)


ATTN_CHAIN = 256
ATTN_CHAINS = 4


ATTN_KEY_BLOCK = 1024


def _attention_kernel(q_ref, kl_ref, kc_ref, vtl_ref, vtc_ref, o_ref):
    l = kl_ref.shape[1]
    lc = kc_ref.shape[1]
    blocks = [(kc_ref, vtc_ref, 0, lc)]
    blocks += [(kl_ref, vtl_ref, r0, ATTN_KEY_BLOCK) for r0 in range(0, l, ATTN_KEY_BLOCK)]
    chains = range(ATTN_CHAINS)
    rows = [slice(ch * ATTN_CHAIN, (ch + 1) * ATTN_CHAIN) for ch in chains]
    q = [q_ref[0, rows[ch], :] for ch in chains]

    def scores(ch, j):
        k_ref, _, r0, n = blocks[j]
        return _dot_nt(k_ref[0, r0:r0 + n, :], q[ch])

    s = [scores(ch, 0) for ch in chains]
    m = [None] * ATTN_CHAINS
    denom = [None] * ATTN_CHAINS
    acc = [None] * ATTN_CHAINS
    for j in range(len(blocks)):
        _, vt_ref, r0, n = blocks[j]
        for ch in chains:
            s_cur = s[ch]
            if j + 1 < len(blocks):
                s[ch] = scores(ch, j + 1)
            m_blk = jnp.max(s_cur, axis=0, keepdims=True)
            if j == 0:
                m[ch] = m_blk
                p = jnp.exp2(s_cur - m_blk)
                denom[ch] = jnp.sum(p, axis=0, keepdims=True)
                acc[ch] = _dot(vt_ref[0, :, r0:r0 + n], p.astype(BF16))
            else:
                m_new = jnp.maximum(m[ch], m_blk)
                alpha = jnp.exp2(m[ch] - m_new)
                p = jnp.exp2(s_cur - m_new)
                denom[ch] = denom[ch] * alpha + jnp.sum(p, axis=0, keepdims=True)
                acc[ch] = acc[ch] * alpha + _dot(vt_ref[0, :, r0:r0 + n], p.astype(BF16))
                m[ch] = m_new
    for ch in chains:
        o_ref[0, rows[ch], :] = (acc[ch] / denom[ch]).T.astype(BF16)


def _attention(q, k_lat, k_ctx, vt_lat, vt_ctx):
    b, l, _ = q.shape
    lc = k_ctx.shape[1]
    tq = ATTN_CHAIN * ATTN_CHAINS
    return pl.pallas_call(
        _attention_kernel,
        out_shape=jax.ShapeDtypeStruct((b, l, N_MLA_HEADS * MLA_V_DIM), BF16),
        grid=(b, N_MLA_HEADS, l // tq),
        in_specs=[pl.BlockSpec((1, tq, MLA_HEAD_PAD), lambda i, h, j: (i, j, h)),
                  pl.BlockSpec((1, l, MLA_HEAD_PAD), lambda i, h, j: (i, 0, h)),
                  pl.BlockSpec((1, lc, MLA_HEAD_PAD), lambda i, h, j: (i, 0, h)),
                  pl.BlockSpec((1, MLA_V_DIM, l), lambda i, h, j: (i, h, 0)),
                  pl.BlockSpec((1, MLA_V_DIM, lc), lambda i, h, j: (i, h, 0))],
        out_specs=pl.BlockSpec((1, tq, MLA_V_DIM), lambda i, h, j: (i, j, h)),
        compiler_params=_cparams(("arbitrary", "arbitrary", "arbitrary")),
        name="attention",
    )(q, k_lat, k_ctx, vt_lat, vt_ctx)


def _merge_kernel(x_ref, ret_ref, sgate_ref, att_ref, sgr_ref, sgm_ref, gate1_ref, shift2_ref, scale2_ref,
                  g2_ref, wr_ref, wm_ref, wo_ref, wrt_ref, brt_ref, x1_ref, h2_ref, logit_ref):
    r = (ret_ref[0].astype(F32) * sgate_ref[0].astype(F32)).astype(BF16)
    merged = (sgr_ref[0].astype(F32) * _dot(r, wr_ref[...])
              + sgm_ref[0].astype(F32) * _dot(att_ref[0], wm_ref[...]))
    y = _dot(merged.astype(BF16), wo_ref[...])
    x1 = x_ref[0] + gate1_ref[0] * y
    x1_ref[0] = x1
    h2 = _norm_mod(x1, g2_ref[...], shift2_ref[0], scale2_ref[0])
    _store_tile_rows(h2_ref, (0,), h2)
    logit_ref[0] = _dot(h2.astype(BF16), wrt_ref[...]) + brt_ref[...]


def _merge(x, ret_o, sgate, att_o, sgr, sgm, gate1, shift2, scale2, g2, wr, wm, wo, wrt, brt):
    b, l, d = x.shape
    tm = MERGE_TILE
    tok = lambda n: pl.BlockSpec((1, tm, n), lambda i, j: (i, j, 0))
    vec = pl.BlockSpec((1, 1, d), lambda i, j: (i, 0, 0))
    return pl.pallas_call(
        _merge_kernel,
        out_shape=[jax.ShapeDtypeStruct((b, l, d), F32),
                   jax.ShapeDtypeStruct((b,) + _tile_rows_shape(l, d), F32),
                   jax.ShapeDtypeStruct((b, l, LANES), F32)],
        grid=(b, l // tm),
        in_specs=[tok(d), tok(ret_o.shape[2]), tok(sgate.shape[2]), tok(att_o.shape[2]), tok(d), tok(d),
                  vec, vec, vec, _resident((1, d)), _resident(wr.shape), _resident(wm.shape),
                  _resident(wo.shape), _resident(wrt.shape), _resident(brt.shape)],
        out_specs=[tok(d), pl.BlockSpec((1,) + _tile_rows_shape(tm, d), lambda i, j: (i, j, 0)), tok(LANES)],
        compiler_params=_cparams(("arbitrary", "arbitrary")),
        name="merge",
    )(x, ret_o, sgate, att_o, sgr, sgm, gate1, shift2, scale2, g2, wr, wm, wo, wrt, brt)


EXPERT_BLOCK = 512
EXPERT_SPLIT = 1


def _expert_kernel(be_ref, nb_ref, x_ref, wgu_ref, bgu_ref, wd_ref, bd_ref, y_ref):
    @pl.when(pl.program_id(0) >= nb_ref[0])
    def _():
        y_ref[...] = jnp.zeros_like(y_ref)

    @pl.when(pl.program_id(0) < nb_ref[0])
    def _():
        f = wd_ref.shape[1]
        x = _load_tile_rows(x_ref, (), EXPERT_BLOCK).astype(BF16)
        fs = f // EXPERT_SPLIT

        def gate_up(j):
            c0 = j * fs
            return (_dot(x, wgu_ref[0, :, c0:c0 + fs]) + bgu_ref[0, :, c0:c0 + fs],
                    _dot(x, wgu_ref[0, :, f + c0:f + c0 + fs]) + bgu_ref[0, :, f + c0:f + c0 + fs])

        y = None
        nxt = gate_up(0)
        for j in range(EXPERT_SPLIT):
            gate, up = nxt
            if j + 1 < EXPERT_SPLIT:
                nxt = gate_up(j + 1)
            gate = jnp.minimum(gate, SWIGLU_LIMIT)
            up = jnp.clip(up, -SWIGLU_LIMIT, SWIGLU_LIMIT)
            glu = gate * _sigmoid(SWIGLU_ALPHA * gate)
            act = ((up + 1.0) * glu).astype(BF16)
            part = _dot(act, wd_ref[0, j * fs:(j + 1) * fs, :])
            y = part if y is None else y + part
        _store_tile_rows(y_ref, (), y + bd_ref[0])


def _experts(block_e, n_used, xs, w_gu, b_gu, w_down, b_down):
    n_rows = xs.shape[0] // SUBLANES
    e, d, f2 = w_gu.shape
    f = f2 // 2
    bm = EXPERT_BLOCK
    blk = _tile_rows_shape(bm, d)
    return pl.pallas_call(
        _expert_kernel,
        out_shape=jax.ShapeDtypeStruct(xs.shape, F32),
        grid_spec=pltpu.PrefetchScalarGridSpec(
            num_scalar_prefetch=2,
            grid=(n_rows // bm,),
            in_specs=[pl.BlockSpec(blk, lambda i, be, nb: (jnp.minimum(i, nb[0] - 1), 0)),
                      pl.BlockSpec((1, d, f2), lambda i, be, nb: (be[i], 0, 0)),
                      pl.BlockSpec((1, 1, f2), lambda i, be, nb: (be[i], 0, 0)),
                      pl.BlockSpec((1, f, d), lambda i, be, nb: (be[i], 0, 0)),
                      pl.BlockSpec((1, 1, d), lambda i, be, nb: (be[i], 0, 0))],
            out_specs=pl.BlockSpec(blk, lambda i, be, nb: (i, 0))),
        compiler_params=_cparams(("arbitrary",)),
        name="experts",
    )(block_e, n_used, xs, w_gu, b_gu.reshape(e, 1, f2), w_down, b_down.reshape(e, 1, d))


ROUTE_TILE = 512
ROUTE_ROWS = 8
DMA_UNROLL = 8


def _router_kernel(logit_ref, idx_ref, gate_ref, rank_ref, cnt_ref, base_ref, tri_ref):
    tm = logit_ref.shape[0]
    step = pl.program_id(0)

    @pl.when(step == 0)
    def _():
        base_ref[...] = jnp.zeros_like(base_ref)
        r = lax.broadcasted_iota(jnp.int32, (tm, tm), 0)
        c = lax.broadcasted_iota(jnp.int32, (tm, tm), 1)
        tri_ref[...] = jnp.where(r < c, 1.0, 0.0).astype(BF16)

    v = logit_ref[...].T[:N_EXPERTS, :]
    eid = lax.broadcasted_iota(jnp.int32, (N_EXPERTS, tm), 0)
    onehot = jnp.zeros((N_EXPERTS, tm), F32)
    vals, sels = [], []
    for kk in range(TOP_K):
        mx = jnp.max(v, axis=0, keepdims=True)
        ik = jnp.min(jnp.where(v == mx, eid, N_EXPERTS), axis=0, keepdims=True)
        sel = eid == ik
        idx_ref[kk:kk + 1, :] = ik
        vals.append(mx)
        sels.append(sel)
        onehot = onehot + jnp.where(sel, 1.0, 0.0)
        v = jnp.where(sel, -jnp.inf, v)
    ex = [jnp.exp(val - vals[0]) for val in vals]
    tot = ex[0] + ex[1] + ex[2] + ex[3]
    for kk in range(TOP_K):
        gate_ref[kk:kk + 1, :] = ex[kk] / tot
    rank_e = base_ref[:, 0:1] + _dot(onehot.astype(BF16), tri_ref[...])
    for kk in range(TOP_K):
        rank_ref[kk:kk + 1, :] = jnp.sum(jnp.where(sels[kk], rank_e, 0.0), axis=0, keepdims=True).astype(jnp.int32)
    zero_i = jnp.zeros((ROUTE_ROWS - TOP_K, tm), jnp.int32)
    idx_ref[TOP_K:, :] = zero_i
    rank_ref[TOP_K:, :] = zero_i
    gate_ref[TOP_K:, :] = jnp.zeros((ROUTE_ROWS - TOP_K, tm), F32)
    base_ref[...] = base_ref[...] + jnp.sum(onehot, axis=1, keepdims=True)
    cnt_ref[...] = base_ref[...]


def _router(logits):
    n_tok = logits.shape[0]
    tm = ROUTE_TILE
    tab = lambda: pl.BlockSpec((ROUTE_ROWS, tm), lambda i: (0, i))
    return pl.pallas_call(
        _router_kernel,
        out_shape=[jax.ShapeDtypeStruct((ROUTE_ROWS, n_tok), jnp.int32),
                   jax.ShapeDtypeStruct((ROUTE_ROWS, n_tok), F32),
                   jax.ShapeDtypeStruct((ROUTE_ROWS, n_tok), jnp.int32),
                   jax.ShapeDtypeStruct((N_EXPERTS, LANES), F32)],
        grid=(n_tok // tm,),
        in_specs=[pl.BlockSpec((tm, LANES), lambda i: (i, 0))],
        out_specs=[tab(), tab(), tab(), pl.BlockSpec((N_EXPERTS, LANES), lambda i: (0, 0))],
        scratch_shapes=[pltpu.VMEM((N_EXPERTS, LANES), F32), pltpu.VMEM((tm, tm), BF16)],
        compiler_params=_cparams(("arbitrary",)),
        name="router",
    )(logits)


def _layout(idx, rank, counts, n_tok):
    bm = EXPERT_BLOCK
    counts = counts[:, 0].astype(jnp.int32)
    padded = (counts + bm - 1) // bm * bm
    pad_end = jnp.cumsum(padded)
    pad_start = pad_end - padded
    n_rows = n_tok * TOP_K + N_EXPERTS * bm
    n_blocks = n_rows // bm
    blk_start = jnp.arange(n_blocks, dtype=jnp.int32) * bm
    block_e = jnp.minimum(jnp.sum(blk_start[:, None] >= pad_end[None, :], axis=1), N_EXPERTS - 1).astype(jnp.int32)
    n_used = (pad_end[-1] // bm).astype(jnp.int32).reshape(1)
    start_of = jnp.zeros_like(idx)
    for e in range(N_EXPERTS):
        start_of = jnp.where(idx == e, pad_start[e], start_of)
    dest = start_of + rank
    fill = jnp.stack([pad_start + counts, pad_end]).astype(jnp.int32)
    return dest, block_e, n_used, fill, n_rows


def _dispatch_kernel(fill_ref, dest_ref, h_ref, xs_ref, zero_ref, sem, zsem):
    tm = h_ref.shape[0] // SUBLANES

    def row_copy(t, kk):
        return pltpu.make_async_copy(_tile_row(h_ref, t), _tile_row(xs_ref, dest_ref[0, 0, t * TOP_K + kk]), sem)

    def issue(t, carry):
        for kk in range(TOP_K):
            row_copy(t, kk).start(priority=kk % 2)
        return carry

    lax.fori_loop(0, tm, issue, 0, unroll=DMA_UNROLL)

    def drain(t, carry):
        for kk in range(TOP_K):
            row_copy(t, kk).wait()
        return carry

    lax.fori_loop(0, tm, drain, 0, unroll=DMA_UNROLL)

    @pl.when(pl.program_id(0) == pl.num_programs(0) - 1)
    def _():
        zero_ref[...] = jnp.zeros_like(zero_ref)

        def zero_copy(r):
            return pltpu.make_async_copy(zero_ref, _tile_row(xs_ref, r), zsem)

        def per_expert(e, carry):
            lo = fill_ref[0, e]
            hi = fill_ref[1, e]
            lax.fori_loop(lo, hi, lambda r, c: (zero_copy(r).start(), c)[1], 0)
            lax.fori_loop(lo, hi, lambda r, c: (zero_copy(r).wait(), c)[1], 0)
            return carry

        lax.fori_loop(0, N_EXPERTS, per_expert, 0)


def _dest_table(dest, tm):
    n_tok = dest.shape[1]
    return dest[:TOP_K].T.reshape(n_tok // tm, 1, tm * TOP_K)


def _dispatch(fill, dest, h2, n_rows):
    n_tok = h2.shape[0] // SUBLANES
    tm = ROUTE_TILE
    return pl.pallas_call(
        _dispatch_kernel,
        out_shape=jax.ShapeDtypeStruct((n_rows * SUBLANES, LANES), h2.dtype),
        grid_spec=pltpu.PrefetchScalarGridSpec(
            num_scalar_prefetch=1,
            grid=(n_tok // tm,),
            in_specs=[pl.BlockSpec((1, 1, tm * TOP_K), lambda i, f: (i, 0, 0), memory_space=pltpu.SMEM),
                      pl.BlockSpec((tm * SUBLANES, LANES), lambda i, f: (i, 0))],
            out_specs=pl.BlockSpec(memory_space=pl.ANY),
            scratch_shapes=[pltpu.VMEM((SUBLANES, LANES), h2.dtype), pltpu.SemaphoreType.DMA(()),
                            pltpu.SemaphoreType.DMA(())]),
        compiler_params=_cparams(("arbitrary",)),
        name="dispatch",
    )(fill, _dest_table(dest, tm), h2)


COMBINE_TILE = 256


def _combine_kernel(dest_ref, dest_next_ref, x1_ref, gates_ref, gate2_ref, gf_ref, ys_ref, o_ref, buf_ref, sem):
    tm = x1_ref.shape[1]
    step = pl.program_id(0)
    slot = lax.rem(step, 2)

    def row_copy(table_ref, sl, t, kk):
        return pltpu.make_async_copy(_tile_row(ys_ref, table_ref[0, 0, t * TOP_K + kk]),
                                     _tile_row(buf_ref.at[sl, kk], t), sem.at[sl])

    def gather(table_ref, sl):
        def issue(t, carry):
            for kk in range(TOP_K):
                row_copy(table_ref, sl, t, kk).start(priority=kk % 2)
            return carry

        lax.fori_loop(0, tm, issue, 0, unroll=DMA_UNROLL)

    @pl.when(step == 0)
    def _():
        gather(dest_ref, slot)

    @pl.when(step + 1 < pl.num_programs(0))
    def _():
        gather(dest_next_ref, 1 - slot)

    def drain(t, carry):
        for kk in range(TOP_K):
            row_copy(dest_ref, slot, t, kk).wait()
        return carry

    lax.fori_loop(0, tm, drain, 0, unroll=DMA_UNROLL)

    gates = gates_ref[...]
    moe = gates[:, 0:1] * _load_tile_rows(buf_ref, (slot, 0), tm)
    for kk in range(1, TOP_K):
        moe = moe + gates[:, kk:kk + 1] * _load_tile_rows(buf_ref, (slot, kk), tm)
    x2 = x1_ref[0] + gate2_ref[0] * moe
    o_ref[0] = _rms(x2) * gf_ref[...]


def _combine(dest, x1, gates_t, gate2, gf, ys):
    b, l, d = x1.shape
    tm = COMBINE_TILE
    per_b = l // tm
    n_tiles = b * per_b
    table = _dest_table(dest, tm)
    table_spec = lambda nxt: pl.BlockSpec((1, 1, tm * TOP_K), lambda g: (jnp.minimum(g + nxt, n_tiles - 1), 0, 0),
                                          memory_space=pltpu.SMEM)
    return pl.pallas_call(
        _combine_kernel,
        out_shape=jax.ShapeDtypeStruct((b, l, d), F32),
        grid=(n_tiles,),
        in_specs=[table_spec(0), table_spec(1),
                  pl.BlockSpec((1, tm, d), lambda g: (g // per_b, g % per_b, 0)),
                  pl.BlockSpec((tm, TOP_K), lambda g: (g, 0)),
                  pl.BlockSpec((1, 1, d), lambda g: (g // per_b, 0, 0)),
                  _resident((1, d)),
                  pl.BlockSpec(memory_space=pl.ANY)],
        out_specs=pl.BlockSpec((1, tm, d), lambda g: (g // per_b, g % per_b, 0)),
        scratch_shapes=[pltpu.VMEM((2, TOP_K) + _tile_rows_shape(tm, d), ys.dtype),
                        pltpu.SemaphoreType.DMA((2,))],
        compiler_params=_cparams(("arbitrary",)),
        name="combine",
    )(table, table, x1, gates_t, gate2, gf, ys)


def kernel(x, c, ctx, c_ctx, norm1_g, norm2_g, ada_w, ada_b, w_in, ret_decay_fwd, ret_decay_bwd, mla_q_norm_g, mla_w_uq, mla_kv_norm_g, mla_w_ukv, w_branch_ret, w_branch_mla, w_out, router_w, router_b, exp_w_gu, exp_b_gu, exp_w_down, exp_b_down, final_norm_g):
    depth = norm1_g.shape[0]
    assert depth == 1, "single-layer block"
    b, l, d = x.shape
    qk_scale = float((MLA_QK_NOPE + MLA_QK_ROPE) ** -0.5 * np.log2(np.e))

    n_mod = b + 1
    rows = -(-n_mod // 8) * 8
    cvec = jnp.concatenate([c, c_ctx[None, :], jnp.zeros((rows - n_mod, d), F32)], axis=0)
    mod = _ada_mod(cvec, ada_w[0], ada_b[0])
    m_lat = [mod[:b, i * d:(i + 1) * d].reshape(b, 1, d) for i in range(6)]
    m_ctx = [mod[b:b + 1, i * d:(i + 1) * d] for i in range(2)]

    w_lat, w_ctx, uq, uk, uvt = _prep_weights(w_in[0], mla_w_uq[0], mla_w_ukv[0])
    tret, tmla = _rope_tables(l, qk_scale)
    g1 = norm1_g[0].reshape(1, d)
    gq = mla_q_norm_g[0].reshape(1, MLA_Q_RANK)
    gkv = mla_kv_norm_g[0].reshape(1, MLA_KV_RANK)

    rq, rk, rv, sgate, q, k_lat, vt_lat, sgr, sgm = _inproj_lat(
        x, m_lat[0], m_lat[1], g1, w_lat, uq, uk, uvt, gq, gkv, tret, tmla, qk_scale)
    rk_ctx, rv_ctx, k_ctx, vt_ctx = _inproj_ctx(ctx, m_ctx[0], m_ctx[1], g1, w_ctx, uk, uvt, gkv)

    lg = jnp.stack([jax.nn.log_sigmoid(ret_decay_fwd[0].astype(F32)),
                    jax.nn.log_sigmoid(ret_decay_bwd[0].astype(F32))])
    ret_o = _retention(lg, rq, rk, rv, rk_ctx, rv_ctx)
    att_o = _attention(q, k_lat, k_ctx, vt_lat, vt_ctx)

    wrt = jnp.pad(router_w[0], ((0, 0), (0, LANES - N_EXPERTS))).astype(BF16)
    brt = jnp.pad(router_b[0], (0, LANES - N_EXPERTS)).reshape(1, LANES)
    x1, h2, logits = _merge(x, ret_o, sgate, att_o, sgr, sgm, m_lat[2], m_lat[3], m_lat[4],
                            norm2_g[0].reshape(1, d), w_branch_ret[0].astype(BF16),
                            w_branch_mla[0].astype(BF16), w_out[0].astype(BF16), wrt, brt)

    n_tok = b * l
    idx, gates, rank, counts = _router(logits.reshape(n_tok, LANES))
    dest, block_e, n_used, fill, n_rows = _layout(idx, rank, counts, n_tok)
    xs = _dispatch(fill, dest, h2.reshape(_tile_rows_shape(n_tok, d)), n_rows)
    ys = _experts(block_e, n_used, xs, exp_w_gu[0].astype(BF16), exp_b_gu[0],
                  exp_w_down[0].astype(BF16), exp_b_down[0])
    return _combine(dest, x1, gates[:TOP_K].T, m_lat[5], final_norm_g.reshape(1, d), ys)
```

```python
import functools

import numpy as np
import jax
import jax.numpy as jnp
from jax import lax
from jax.experimental import pallas as pl
from jax.experimental.pallas import tpu as pltpu

GRID_W = 64
N_RET_HEADS = 4
RET_QK_DIM = 256
RET_V_DIM = 512
N_MLA_HEADS = 8
MLA_Q_RANK = 384
MLA_KV_RANK = 256
MLA_QK_NOPE = 128
MLA_QK_ROPE = 64
MLA_V_DIM = 128
N_EXPERTS = 32
TOP_K = 4
SWIGLU_LIMIT = 7.0
SWIGLU_ALPHA = 1.702
ROPE_BASE = 10000.0
EPS = 1e-6

LANES = 128
MLA_HEAD_PAD = 2 * LANES
RET_CHUNK = 256
VMEM_LIMIT = 56 * 1024 * 1024
INPROJ_TILE = 256
MERGE_TILE = 256

F32 = jnp.float32
BF16 = jnp.bfloat16


def _cparams(sem):
    return pltpu.CompilerParams(dimension_semantics=sem, vmem_limit_bytes=VMEM_LIMIT)


def _resident(shape):
    nd = len(shape)
    return pl.BlockSpec(shape, lambda *_: (0,) * nd, pipeline_mode=pl.Buffered(1))


def _dot(a, b):
    return jnp.dot(a, b, preferred_element_type=F32)


def _dot_nt(a, b):
    return lax.dot_general(a, b, (((1,), (1,)), ((), ())), preferred_element_type=F32)


def _dot_tn(a, b):
    return lax.dot_general(a, b, (((0,), (0,)), ((), ())), preferred_element_type=F32)


SUBLANES = 8


def _tile_rows_shape(n, d):
    assert d == SUBLANES * LANES
    return (n * SUBLANES, LANES)


def _store_tile_rows(ref, lead, val):
    rows = val.shape[0]
    for s in range(SUBLANES):
        ref[lead + (pl.ds(s, rows, stride=SUBLANES), slice(None))] = val[:, s * LANES:(s + 1) * LANES]


def _load_tile_rows(ref, lead, rows):
    return jnp.concatenate([ref[lead + (pl.ds(s, rows, stride=SUBLANES), slice(None))]
                            for s in range(SUBLANES)], axis=1)


def _tile_row(ref, r):
    return ref.at[pl.ds(pl.multiple_of(r * SUBLANES, SUBLANES), SUBLANES), :]


def _rms(x):
    return x * lax.rsqrt(jnp.mean(x * x, axis=-1, keepdims=True) + EPS)


def _sigmoid(x):
    return 1.0 / (1.0 + jnp.exp(-x))


def _ada_kernel(c_ref, w_ref, b_ref, o_ref):
    c = c_ref[...]
    s = c * _sigmoid(c)
    o_ref[...] = jnp.dot(s, w_ref[...], preferred_element_type=F32,
                         precision=lax.Precision.HIGHEST) + b_ref[...]


def _ada_mod(cvec, w, b):
    rows, d = cvec.shape
    n = w.shape[1]
    tn = 1024
    return pl.pallas_call(
        _ada_kernel,
        out_shape=jax.ShapeDtypeStruct((rows, n), F32),
        grid=(n // tn,),
        in_specs=[pl.BlockSpec((rows, d), lambda j: (0, 0)),
                  pl.BlockSpec((d, tn), lambda j: (0, j)),
                  pl.BlockSpec((1, tn), lambda j: (0, j))],
        out_specs=pl.BlockSpec((rows, tn), lambda j: (0, j)),
        compiler_params=_cparams(("arbitrary",)),
        name="ada_mod",
    )(cvec, w, b.reshape(1, n))


def _rope_tables(seq_len, qk_scale):
    pos = np.arange(seq_len)
    rows = (pos // GRID_W).astype(np.float32)
    cols = (pos % GRID_W).astype(np.float32)

    def angles(p, half):
        freqs = (np.float32(ROPE_BASE) ** (-np.arange(half, dtype=np.float32) / np.float32(half))).astype(np.float32)
        return (p[:, None] * freqs[None, :]).astype(np.float32)

    a_r, a_c = angles(rows, 64), angles(cols, 64)
    ret = np.concatenate([np.cos(a_r), np.cos(a_r), np.cos(a_c), np.cos(a_c),
                          -np.sin(a_r), np.sin(a_r), -np.sin(a_c), np.sin(a_c)], axis=1)
    b_r, b_c = angles(rows, 16), angles(cols, 16)
    z32 = np.zeros((seq_len, 32), np.float32)
    z64 = np.zeros((seq_len, 64), np.float32)
    cos = np.concatenate([np.cos(b_r), np.cos(b_c), np.cos(b_r), np.cos(b_c), z64], axis=1)
    sin_up = np.concatenate([-np.sin(b_r), -np.sin(b_c), z32, z64], axis=1)
    sin_dn = np.concatenate([z32, np.sin(b_r), np.sin(b_c), z64], axis=1)
    mla_k = np.concatenate([cos, sin_up, sin_dn], axis=1)
    mla = np.concatenate([mla_k, mla_k * np.float32(qk_scale)], axis=1)
    return jnp.asarray(ret, F32), jnp.asarray(mla, F32)


def _rope_ret(t, tab, parity):
    cos = tab[:, parity * LANES:(parity + 1) * LANES]
    sin = tab[:, (2 + parity) * LANES:(3 + parity) * LANES]
    return t * cos + pltpu.roll(t, 64, 1) * sin


def _rope_mla(t, tab, base):
    cos = tab[:, base:base + LANES]
    sin_up = tab[:, base + LANES:base + 2 * LANES]
    sin_dn = tab[:, base + 2 * LANES:base + 3 * LANES]
    return t * cos + pltpu.roll(t, 96, 1) * sin_up + pltpu.roll(t, 32, 1) * sin_dn


_C_Q, _C_K, _C_V, _C_G = 0, 1024, 2048, 4096
_C_CQ, _C_CKV, _C_KR, _C_GR, _C_GM, _C_END = 6144, 6528, 6784, 6912, 7936, 8960
_X_K, _X_V, _X_CKV, _X_KR, _X_END = 0, 1024, 3072, 3328, 3456


def _norm_mod(x, g, shift, scale):
    return (_rms(x) * g) * (1.0 + scale) + shift


def _mla_kv(ckv_acc, kr, gkv_ref, wuk_ref, wuvt_ref, k_ref, vt_ref):
    ckvn = (_rms(ckv_acc) * gkv_ref[...]).astype(BF16)
    kn = _dot(ckvn, wuk_ref[...])
    krb = kr.astype(BF16)
    for hh in range(N_MLA_HEADS):
        k_ref[0, :, hh * MLA_HEAD_PAD:hh * MLA_HEAD_PAD + LANES] = kn[:, hh * LANES:(hh + 1) * LANES].astype(BF16)
        k_ref[0, :, hh * MLA_HEAD_PAD + LANES:(hh + 1) * MLA_HEAD_PAD] = krb
    vt_ref[0] = _dot_nt(wuvt_ref[...], ckvn).astype(BF16)


def _inproj_lat_kernel(qk_scale, x_ref, shift_ref, scale_ref, g_ref, w_ref, wuq_ref, wuk_ref, wuvt_ref, gq_ref,
                       gkv_ref, tret_ref, tmla_ref,
                       rq_ref, rk_ref, rv_ref, sgate_ref, q_ref, k_ref, vt_ref, sgr_ref, sgm_ref):
    h = _norm_mod(x_ref[0], g_ref[...], shift_ref[0], scale_ref[0]).astype(BF16)
    tret = tret_ref[...]
    tmla = tmla_ref[...]

    def proj(c0, n):
        return _dot(h, w_ref[:, c0:c0 + n])

    for base, out in ((_C_Q, rq_ref), (_C_K, rk_ref)):
        for j in range(2):
            acc = proj(base + j * 512, 512)
            for blk in range(4):
                t = acc[:, blk * LANES:(blk + 1) * LANES]
                col = j * 512 + blk * LANES
                out[0, :, col:col + LANES] = _rope_ret(t, tret, blk % 2).astype(BF16)
    for j in range(4):
        rv_ref[0, :, j * 512:(j + 1) * 512] = proj(_C_V + j * 512, 512).astype(BF16)
    for j in range(4):
        a = proj(_C_G + j * 512, 512)
        sgate_ref[0, :, j * 512:(j + 1) * 512] = (a * _sigmoid(a)).astype(BF16)
    for j in range(2):
        a = proj(_C_GR + j * 512, 512)
        sgr_ref[0, :, j * 512:(j + 1) * 512] = _sigmoid(a).astype(BF16)
    for j in range(2):
        a = proj(_C_GM + j * 512, 512)
        sgm_ref[0, :, j * 512:(j + 1) * 512] = _sigmoid(a).astype(BF16)

    cqn = (_rms(proj(_C_CQ, MLA_Q_RANK)) * gq_ref[...]).astype(BF16)
    for j in range(4):
        acc = _dot(cqn, wuq_ref[:, j * 512:(j + 1) * 512])
        for blk in range(4):
            t = acc[:, blk * LANES:(blk + 1) * LANES]
            col = j * 512 + blk * LANES
            if blk % 2 == 0:
                q_ref[0, :, col:col + LANES] = (t * qk_scale).astype(BF16)
            else:
                q_ref[0, :, col:col + LANES] = _rope_mla(t, tmla, 3 * LANES).astype(BF16)
    kr = _rope_mla(proj(_C_KR, LANES), tmla, 0)
    _mla_kv(proj(_C_CKV, MLA_KV_RANK), kr, gkv_ref, wuk_ref, wuvt_ref, k_ref, vt_ref)


def _inproj_ctx_kernel(x_ref, shift_ref, scale_ref, g_ref, w_ref, wuk_ref, wuvt_ref, gkv_ref,
                       rk_ref, rv_ref, k_ref, vt_ref):
    h = _norm_mod(x_ref[0], g_ref[...], shift_ref[...], scale_ref[...]).astype(BF16)

    def proj(c0, n):
        return _dot(h, w_ref[:, c0:c0 + n])

    for j in range(2):
        rk_ref[0, :, j * 512:(j + 1) * 512] = proj(_X_K + j * 512, 512).astype(BF16)
    for j in range(4):
        rv_ref[0, :, j * 512:(j + 1) * 512] = proj(_X_V + j * 512, 512).astype(BF16)
    _mla_kv(proj(_X_CKV, MLA_KV_RANK), proj(_X_KR, LANES), gkv_ref, wuk_ref, wuvt_ref, k_ref, vt_ref)


def _prep_weights(w_in, w_uq, w_ukv):
    nq = N_RET_HEADS * RET_QK_DIM
    nv = N_RET_HEADS * RET_V_DIM
    d = w_in.shape[0]
    sizes = (nq, nq, nv, nv, MLA_Q_RANK, MLA_KV_RANK, MLA_QK_ROPE, d, d)
    offs = np.cumsum((0,) + sizes)
    wq, wk, wv, wg, wcq, wckv, wkr, wgr, wgm = [w_in[:, offs[i]:offs[i + 1]] for i in range(9)]
    wk = wk * (RET_QK_DIM ** -0.5)
    perm = np.concatenate([np.arange(0, 16), np.arange(32, 48), np.arange(16, 32), np.arange(48, 64)])
    wkr = jnp.pad(wkr[:, perm], ((0, 0), (0, LANES - MLA_QK_ROPE)))
    w_lat = jnp.concatenate([wq, wk, wv, wg, wcq, wckv, wkr, wgr, wgm], axis=1).astype(BF16)
    w_ctx = jnp.concatenate([wk, wv, wckv, wkr], axis=1).astype(BF16)
    uq = w_uq.reshape(MLA_Q_RANK, N_MLA_HEADS, MLA_QK_NOPE + MLA_QK_ROPE)
    uq = jnp.concatenate([uq[:, :, :MLA_QK_NOPE], uq[:, :, MLA_QK_NOPE:][:, :, perm],
                          jnp.zeros((MLA_Q_RANK, N_MLA_HEADS, LANES - MLA_QK_ROPE), w_uq.dtype)], axis=2)
    uq = uq.reshape(MLA_Q_RANK, N_MLA_HEADS * MLA_HEAD_PAD).astype(BF16)
    ukv = w_ukv.reshape(MLA_KV_RANK, N_MLA_HEADS, MLA_QK_NOPE + MLA_V_DIM)
    uk = ukv[:, :, :MLA_QK_NOPE].reshape(MLA_KV_RANK, -1).astype(BF16)
    uvt = ukv[:, :, MLA_QK_NOPE:].reshape(MLA_KV_RANK, -1).T.astype(BF16)
    return w_lat, w_ctx, uq, uk, uvt


def _tok_spec(tm, n):
    return pl.BlockSpec((1, tm, n), lambda i, j: (i, j, 0))


def _vt_spec(tm):
    return pl.BlockSpec((1, N_MLA_HEADS * MLA_V_DIM, tm), lambda i, j: (i, 0, j))


def _inproj_lat(x, shift, scale, g, w_lat, uq, uk, uvt, gq, gkv, tret, tmla, qk_scale):
    b, l, d = x.shape
    tm = INPROJ_TILE
    vec = pl.BlockSpec((1, 1, d), lambda i, j: (i, 0, 0))
    out_w = (1024, 1024, 2048, 2048, 2048, 2048, None, 1024, 1024)
    nvt = N_MLA_HEADS * MLA_V_DIM
    return pl.pallas_call(
        functools.partial(_inproj_lat_kernel, qk_scale),
        out_shape=[jax.ShapeDtypeStruct((b, nvt, l) if n is None else (b, l, n), BF16) for n in out_w],
        grid=(b, l // tm),
        in_specs=[_tok_spec(tm, d), vec, vec, _resident((1, d)), _resident(w_lat.shape), _resident(uq.shape),
                  _resident(uk.shape), _resident(uvt.shape), _resident((1, MLA_Q_RANK)),
                  _resident((1, MLA_KV_RANK)),
                  pl.BlockSpec((tm, tret.shape[1]), lambda i, j: (j, 0)),
                  pl.BlockSpec((tm, tmla.shape[1]), lambda i, j: (j, 0))],
        out_specs=[_vt_spec(tm) if n is None else _tok_spec(tm, n) for n in out_w],
        compiler_params=_cparams(("arbitrary", "arbitrary")),
        name="inproj_lat",
    )(x, shift, scale, g, w_lat, uq, uk, uvt, gq, gkv, tret, tmla)


def _inproj_ctx(ctx, shift, scale, g, w_ctx, uk, uvt, gkv):
    b, l, d = ctx.shape
    tm = 256
    out_w = (1024, 2048, 2048, None)
    nvt = N_MLA_HEADS * MLA_V_DIM
    return pl.pallas_call(
        _inproj_ctx_kernel,
        out_shape=[jax.ShapeDtypeStruct((b, nvt, l) if n is None else (b, l, n), BF16) for n in out_w],
        grid=(b, l // tm),
        in_specs=[_tok_spec(tm, d), _resident((1, d)), _resident((1, d)), _resident((1, d)),
                  _resident(w_ctx.shape), _resident(uk.shape), _resident(uvt.shape),
                  _resident((1, MLA_KV_RANK))],
        out_specs=[_vt_spec(tm) if n is None else _tok_spec(tm, n) for n in out_w],
        compiler_params=_cparams(("arbitrary", "arbitrary")),
        name="inproj_ctx",
    )(ctx, shift, scale, g, w_ctx, uk, uvt, gkv)


def _retention_kernel(n_chunks, n_ctx_chunks, lg_ref, q_ref, k_ref, v_ref, kc_ref, vc_ref, o_ref,
                      ob_ref, sf_ref, sb_ref, dmat_ref, dec_ref):
    c = RET_CHUNK
    head = pl.program_id(1)
    lgf = lg_ref[0, head]
    lgb = lg_ref[1, head]
    ri = lax.broadcasted_iota(jnp.int32, (c, c), 0).astype(F32)
    ci = lax.broadcasted_iota(jnp.int32, (c, c), 1).astype(F32)
    diff = ri - ci
    dmat_ref[...] = jnp.where(diff >= 0, jnp.exp(lgf * jnp.maximum(diff, 0.0)),
                              jnp.exp(lgb * jnp.maximum(-diff, 0.0)))
    rk = lax.broadcasted_iota(jnp.int32, (c, RET_QK_DIM), 0).astype(F32)
    dec_ref[0] = jnp.exp(lgf * (rk + 1.0))
    dec_ref[1] = jnp.exp(lgf * (c - 1.0 - rk))
    dec_ref[2] = jnp.exp(lgb * (c - rk))
    dec_ref[3] = jnp.exp(lgb * rk)
    cdf = jnp.exp(jnp.full((1, RET_V_DIM), lgf * c, F32))
    cdb = jnp.exp(jnp.full((1, RET_V_DIM), lgb * c, F32))

    def scaled(t, which):
        return (t.astype(F32) * dec_ref[which]).astype(BF16)

    sf_ref[...] = jnp.zeros_like(sf_ref)
    sb_ref[...] = jnp.zeros_like(sb_ref)
    for n in range(n_ctx_chunks):
        sl = slice(n * c, (n + 1) * c)
        sf_ref[...] = sf_ref[...] * cdf + _dot_tn(scaled(kc_ref[0, sl, :], 1), vc_ref[0, sl, :])
    for n in reversed(range(n_ctx_chunks)):
        sl = slice(n * c, (n + 1) * c)
        sb_ref[...] = sb_ref[...] * cdb + _dot_tn(scaled(kc_ref[0, sl, :], 3), vc_ref[0, sl, :])

    def step(i, finish):
        rb = pl.multiple_of((n_chunks - 1 - i) * c, c)
        rf = pl.multiple_of(i * c, c)
        qf = q_ref[0, pl.ds(rf, c), :]
        kf = k_ref[0, pl.ds(rf, c), :]
        vf = v_ref[0, pl.ds(rf, c), :]
        qb = q_ref[0, pl.ds(rb, c), :]
        kb = k_ref[0, pl.ds(rb, c), :]
        vb = v_ref[0, pl.ds(rb, c), :]
        s = _dot_nt(qf, kf)
        o_b = _dot(scaled(qb, 2), sb_ref[...].astype(BF16))
        sb_new = _dot_tn(scaled(kb, 3), vb)
        o_f = _dot(scaled(qf, 0), sf_ref[...].astype(BF16))
        sf_new = _dot_tn(scaled(kf, 1), vf)
        o_f = o_f + _dot((s * dmat_ref[...]).astype(BF16), vf)
        sb_ref[...] = sb_ref[...] * cdb + sb_new
        sf_ref[...] = sf_ref[...] * cdf + sf_new
        if finish:
            o_ref[0, pl.ds(rb, c), :] = _rms(o_b + ob_ref[pl.ds(rb, c), :]).astype(BF16)
            o_ref[0, pl.ds(rf, c), :] = _rms(o_f + ob_ref[pl.ds(rf, c), :]).astype(BF16)
        else:
            ob_ref[pl.ds(rb, c), :] = o_b
            ob_ref[pl.ds(rf, c), :] = o_f

    half = n_chunks // 2
    lax.fori_loop(0, half, lambda i, carry: (step(i, False), carry)[1], 0)
    lax.fori_loop(half, n_chunks, lambda i, carry: (step(i, True), carry)[1], 0)


def _retention(lg, rq, rk, rv, rk_ctx, rv_ctx):
    b, l, _ = rq.shape
    lc = rk_ctx.shape[1]
    c = RET_CHUNK
    assert l % (2 * c) == 0 and lc % c == 0
    qk = lambda n: pl.BlockSpec((1, n, RET_QK_DIM), lambda i, h: (i, 0, h))
    vv = lambda n: pl.BlockSpec((1, n, RET_V_DIM), lambda i, h: (i, 0, h))
    return pl.pallas_call(
        functools.partial(_retention_kernel, l // c, lc // c),
        out_shape=jax.ShapeDtypeStruct((b, l, N_RET_HEADS * RET_V_DIM), BF16),
        grid=(b, N_RET_HEADS),
        in_specs=[pl.BlockSpec(memory_space=pltpu.SMEM), qk(l), qk(l), vv(l), qk(lc), vv(lc)],
        out_specs=vv(l),
        scratch_shapes=[pltpu.VMEM((l, RET_V_DIM), F32),
                        pltpu.VMEM((RET_QK_DIM, RET_V_DIM), F32),
                        pltpu.VMEM((RET_QK_DIM, RET_V_DIM), F32),
                        pltpu.VMEM((c, c), F32),
                        pltpu.VMEM((4, c, RET_QK_DIM), F32)],
        compiler_params=_cparams(("arbitrary", "arbitrary")),
        name="retention",
    )(lg, rq, rk, rv, rk_ctx, rv_ctx)


ATTN_CHAIN = 256
ATTN_CHAINS = 8


ATTN_KEY_BLOCK = 1024


def _attention_kernel(q_ref, kl_ref, kc_ref, vtl_ref, vtc_ref, o_ref):
    l = kl_ref.shape[1]
    lc = kc_ref.shape[1]
    blocks = [(kc_ref, vtc_ref, 0, lc)]
    blocks += [(kl_ref, vtl_ref, r0, ATTN_KEY_BLOCK) for r0 in range(0, l, ATTN_KEY_BLOCK)]
    chains = range(ATTN_CHAINS)
    rows = [slice(ch * ATTN_CHAIN, (ch + 1) * ATTN_CHAIN) for ch in chains]
    q = [q_ref[0, rows[ch], :] for ch in chains]

    def scores(ch, j):
        k_ref, _, r0, n = blocks[j]
        return _dot_nt(k_ref[0, r0:r0 + n, :], q[ch])

    s = [scores(ch, 0) for ch in chains]
    m = [None] * ATTN_CHAINS
    denom = [None] * ATTN_CHAINS
    acc = [None] * ATTN_CHAINS
    for j in range(len(blocks)):
        _, vt_ref, r0, n = blocks[j]
        for ch in chains:
            s_cur = s[ch]
            if j + 1 < len(blocks):
                s[ch] = scores(ch, j + 1)
            m_blk = jnp.max(s_cur, axis=0, keepdims=True)
            if j == 0:
                m[ch] = m_blk
                p = jnp.exp2(s_cur - m_blk)
                denom[ch] = jnp.sum(p, axis=0, keepdims=True)
                acc[ch] = _dot(vt_ref[0, :, r0:r0 + n], p.astype(BF16))
            else:
                m_new = jnp.maximum(m[ch], m_blk)
                alpha = jnp.exp2(m[ch] - m_new)
                p = jnp.exp2(s_cur - m_new)
                denom[ch] = denom[ch] * alpha + jnp.sum(p, axis=0, keepdims=True)
                acc[ch] = acc[ch] * alpha + _dot(vt_ref[0, :, r0:r0 + n], p.astype(BF16))
                m[ch] = m_new
    for ch in chains:
        o_ref[0, rows[ch], :] = (acc[ch] / denom[ch]).T.astype(BF16)


def _attention(q, k_lat, k_ctx, vt_lat, vt_ctx):
    b, l, _ = q.shape
    lc = k_ctx.shape[1]
    tq = ATTN_CHAIN * ATTN_CHAINS
    return pl.pallas_call(
        _attention_kernel,
        out_shape=jax.ShapeDtypeStruct((b, l, N_MLA_HEADS * MLA_V_DIM), BF16),
        grid=(b, N_MLA_HEADS, l // tq),
        in_specs=[pl.BlockSpec((1, tq, MLA_HEAD_PAD), lambda i, h, j: (i, j, h)),
                  pl.BlockSpec((1, l, MLA_HEAD_PAD), lambda i, h, j: (i, 0, h)),
                  pl.BlockSpec((1, lc, MLA_HEAD_PAD), lambda i, h, j: (i, 0, h)),
                  pl.BlockSpec((1, MLA_V_DIM, l), lambda i, h, j: (i, h, 0)),
                  pl.BlockSpec((1, MLA_V_DIM, lc), lambda i, h, j: (i, h, 0))],
        out_specs=pl.BlockSpec((1, tq, MLA_V_DIM), lambda i, h, j: (i, j, h)),
        compiler_params=_cparams(("arbitrary", "arbitrary", "arbitrary")),
        name="attention",
    )(q, k_lat, k_ctx, vt_lat, vt_ctx)


def _merge_kernel(x_ref, ret_ref, sgate_ref, att_ref, sgr_ref, sgm_ref, gate1_ref, shift2_ref, scale2_ref,
                  g2_ref, wr_ref, wm_ref, wo_ref, wrt_ref, brt_ref, x1_ref, h2_ref, logit_ref):
    r = (ret_ref[0].astype(F32) * sgate_ref[0].astype(F32)).astype(BF16)
    merged = (sgr_ref[0].astype(F32) * _dot(r, wr_ref[...])
              + sgm_ref[0].astype(F32) * _dot(att_ref[0], wm_ref[...]))
    y = _dot(merged.astype(BF16), wo_ref[...])
    x1 = x_ref[0] + gate1_ref[0] * y
    x1_ref[0] = x1
    h2 = _norm_mod(x1, g2_ref[...], shift2_ref[0], scale2_ref[0])
    _store_tile_rows(h2_ref, (0,), h2)
    logit_ref[0] = _dot(h2.astype(BF16), wrt_ref[...]) + brt_ref[...]


def _merge(x, ret_o, sgate, att_o, sgr, sgm, gate1, shift2, scale2, g2, wr, wm, wo, wrt, brt):
    b, l, d = x.shape
    tm = MERGE_TILE
    tok = lambda n: pl.BlockSpec((1, tm, n), lambda i, j: (i, j, 0))
    vec = pl.BlockSpec((1, 1, d), lambda i, j: (i, 0, 0))
    return pl.pallas_call(
        _merge_kernel,
        out_shape=[jax.ShapeDtypeStruct((b, l, d), F32),
                   jax.ShapeDtypeStruct((b,) + _tile_rows_shape(l, d), F32),
                   jax.ShapeDtypeStruct((b, l, LANES), F32)],
        grid=(b, l // tm),
        in_specs=[tok(d), tok(ret_o.shape[2]), tok(sgate.shape[2]), tok(att_o.shape[2]), tok(d), tok(d),
                  vec, vec, vec, _resident((1, d)), _resident(wr.shape), _resident(wm.shape),
                  _resident(wo.shape), _resident(wrt.shape), _resident(brt.shape)],
        out_specs=[tok(d), pl.BlockSpec((1,) + _tile_rows_shape(tm, d), lambda i, j: (i, j, 0)), tok(LANES)],
        compiler_params=_cparams(("arbitrary", "arbitrary")),
        name="merge",
    )(x, ret_o, sgate, att_o, sgr, sgm, gate1, shift2, scale2, g2, wr, wm, wo, wrt, brt)


EXPERT_BLOCK = 512


def _expert_kernel(be_ref, nb_ref, x_ref, wgu_ref, bgu_ref, wd_ref, bd_ref, y_ref, wgu_bf, wd_bf):
    step = pl.program_id(0)

    @pl.when(step >= nb_ref[0])
    def _():
        y_ref[...] = jnp.zeros_like(y_ref)

    @pl.when((step < nb_ref[0]) & ((step == 0) | (be_ref[step] != be_ref[jnp.maximum(step - 1, 0)])))
    def _():
        cw = 4 * LANES
        for c0 in range(0, wgu_bf.shape[1], cw):
            wgu_bf[:, c0:c0 + cw] = wgu_ref[0, :, c0:c0 + cw].astype(BF16)
        for c0 in range(0, wd_bf.shape[1], cw):
            wd_bf[:, c0:c0 + cw] = wd_ref[0, :, c0:c0 + cw].astype(BF16)

    @pl.when(step < nb_ref[0])
    def _():
        f = wd_bf.shape[0]
        x = _load_tile_rows(x_ref, (), EXPERT_BLOCK).astype(BF16)
        gate = _dot(x, wgu_bf[:, :f]) + bgu_ref[0, :, :f]
        up = _dot(x, wgu_bf[:, f:]) + bgu_ref[0, :, f:]
        gate = jnp.minimum(gate, SWIGLU_LIMIT)
        up = jnp.clip(up, -SWIGLU_LIMIT, SWIGLU_LIMIT)
        glu = gate * _sigmoid(SWIGLU_ALPHA * gate)
        act = ((up + 1.0) * glu).astype(BF16)
        _store_tile_rows(y_ref, (), _dot(act, wd_bf[...]) + bd_ref[0])


def _experts(block_e, n_used, xs, w_gu, b_gu, w_down, b_down):
    n_rows = xs.shape[0] // SUBLANES
    e, d, f2 = w_gu.shape
    f = f2 // 2
    bm = EXPERT_BLOCK
    blk = _tile_rows_shape(bm, d)
    return pl.pallas_call(
        _expert_kernel,
        out_shape=jax.ShapeDtypeStruct(xs.shape, F32),
        grid_spec=pltpu.PrefetchScalarGridSpec(
            num_scalar_prefetch=2,
            grid=(n_rows // bm,),
            in_specs=[pl.BlockSpec(blk, lambda i, be, nb: (jnp.minimum(i, nb[0] - 1), 0)),
                      pl.BlockSpec((1, d, f2), lambda i, be, nb: (be[i], 0, 0)),
                      pl.BlockSpec((1, 1, f2), lambda i, be, nb: (be[i], 0, 0)),
                      pl.BlockSpec((1, f, d), lambda i, be, nb: (be[i], 0, 0)),
                      pl.BlockSpec((1, 1, d), lambda i, be, nb: (be[i], 0, 0))],
            out_specs=pl.BlockSpec(blk, lambda i, be, nb: (i, 0)),
            scratch_shapes=[pltpu.VMEM((d, f2), BF16), pltpu.VMEM((f, d), BF16)]),
        compiler_params=_cparams(("arbitrary",)),
        name="experts",
    )(block_e, n_used, xs, w_gu, b_gu.reshape(e, 1, f2), w_down, b_down.reshape(e, 1, d))


ROUTE_TILE = 512
ROUTE_ROWS = 8
DMA_UNROLL = 8


def _router_kernel(logit_ref, idx_ref, gate_ref, rank_ref, cnt_ref, base_ref, tri_ref):
    tm = logit_ref.shape[0]
    step = pl.program_id(0)

    @pl.when(step == 0)
    def _():
        base_ref[...] = jnp.zeros_like(base_ref)
        r = lax.broadcasted_iota(jnp.int32, (tm, tm), 0)
        c = lax.broadcasted_iota(jnp.int32, (tm, tm), 1)
        tri_ref[...] = jnp.where(r < c, 1.0, 0.0).astype(BF16)

    v = logit_ref[...].T[:N_EXPERTS, :]
    eid = lax.broadcasted_iota(jnp.int32, (N_EXPERTS, tm), 0)
    onehot = jnp.zeros((N_EXPERTS, tm), F32)
    vals, sels = [], []
    for kk in range(TOP_K):
        mx = jnp.max(v, axis=0, keepdims=True)
        ik = jnp.min(jnp.where(v == mx, eid, N_EXPERTS), axis=0, keepdims=True)
        sel = eid == ik
        idx_ref[kk:kk + 1, :] = ik
        vals.append(mx)
        sels.append(sel)
        onehot = onehot + jnp.where(sel, 1.0, 0.0)
        v = jnp.where(sel, -jnp.inf, v)
    ex = [jnp.exp(val - vals[0]) for val in vals]
    tot = ex[0] + ex[1] + ex[2] + ex[3]
    for kk in range(TOP_K):
        gate_ref[kk:kk + 1, :] = ex[kk] / tot
    rank_e = base_ref[:, 0:1] + _dot(onehot.astype(BF16), tri_ref[...])
    for kk in range(TOP_K):
        rank_ref[kk:kk + 1, :] = jnp.sum(jnp.where(sels[kk], rank_e, 0.0), axis=0, keepdims=True).astype(jnp.int32)
    zero_i = jnp.zeros((ROUTE_ROWS - TOP_K, tm), jnp.int32)
    idx_ref[TOP_K:, :] = zero_i
    rank_ref[TOP_K:, :] = zero_i
    gate_ref[TOP_K:, :] = jnp.zeros((ROUTE_ROWS - TOP_K, tm), F32)
    base_ref[...] = base_ref[...] + jnp.sum(onehot, axis=1, keepdims=True)
    cnt_ref[...] = base_ref[...]


def _router(logits):
    n_tok = logits.shape[0]
    tm = ROUTE_TILE
    tab = lambda: pl.BlockSpec((ROUTE_ROWS, tm), lambda i: (0, i))
    return pl.pallas_call(
        _router_kernel,
        out_shape=[jax.ShapeDtypeStruct((ROUTE_ROWS, n_tok), jnp.int32),
                   jax.ShapeDtypeStruct((ROUTE_ROWS, n_tok), F32),
                   jax.ShapeDtypeStruct((ROUTE_ROWS, n_tok), jnp.int32),
                   jax.ShapeDtypeStruct((N_EXPERTS, LANES), F32)],
        grid=(n_tok // tm,),
        in_specs=[pl.BlockSpec((tm, LANES), lambda i: (i, 0))],
        out_specs=[tab(), tab(), tab(), pl.BlockSpec((N_EXPERTS, LANES), lambda i: (0, 0))],
        scratch_shapes=[pltpu.VMEM((N_EXPERTS, LANES), F32), pltpu.VMEM((tm, tm), BF16)],
        compiler_params=_cparams(("arbitrary",)),
        name="router",
    )(logits)


def _layout(idx, rank, counts, n_tok):
    bm = EXPERT_BLOCK
    counts = counts[:, 0].astype(jnp.int32)
    padded = (counts + bm - 1) // bm * bm
    pad_end = jnp.cumsum(padded)
    pad_start = pad_end - padded
    n_rows = n_tok * TOP_K + N_EXPERTS * bm
    n_blocks = n_rows // bm
    blk_start = jnp.arange(n_blocks, dtype=jnp.int32) * bm
    block_e = jnp.minimum(jnp.sum(blk_start[:, None] >= pad_end[None, :], axis=1), N_EXPERTS - 1).astype(jnp.int32)
    n_used = (pad_end[-1] // bm).astype(jnp.int32).reshape(1)
    start_of = jnp.zeros_like(idx)
    for e in range(N_EXPERTS):
        start_of = jnp.where(idx == e, pad_start[e], start_of)
    dest = start_of + rank
    fill = jnp.stack([pad_start + counts, pad_end]).astype(jnp.int32)
    return dest, block_e, n_used, fill, n_rows


def _dispatch_kernel(fill_ref, dest_ref, h_ref, xs_ref, zero_ref, sem, zsem):
    tm = h_ref.shape[0] // SUBLANES

    def row_copy(t, kk):
        return pltpu.make_async_copy(_tile_row(h_ref, t), _tile_row(xs_ref, dest_ref[0, 0, t * TOP_K + kk]), sem)

    def issue(t, carry):
        for kk in range(TOP_K):
            row_copy(t, kk).start(priority=kk % 2)
        return carry

    lax.fori_loop(0, tm, issue, 0, unroll=DMA_UNROLL)

    def drain(t, carry):
        for kk in range(TOP_K):
            row_copy(t, kk).wait()
        return carry

    lax.fori_loop(0, tm, drain, 0, unroll=DMA_UNROLL)

    @pl.when(pl.program_id(0) == pl.num_programs(0) - 1)
    def _():
        zero_ref[...] = jnp.zeros_like(zero_ref)

        def zero_copy(r):
            return pltpu.make_async_copy(zero_ref, _tile_row(xs_ref, r), zsem)

        def per_expert(e, carry):
            lo = fill_ref[0, e]
            hi = fill_ref[1, e]
            lax.fori_loop(lo, hi, lambda r, c: (zero_copy(r).start(), c)[1], 0)
            lax.fori_loop(lo, hi, lambda r, c: (zero_copy(r).wait(), c)[1], 0)
            return carry

        lax.fori_loop(0, N_EXPERTS, per_expert, 0)


def _dest_table(dest, tm):
    n_tok = dest.shape[1]
    return dest[:TOP_K].T.reshape(n_tok // tm, 1, tm * TOP_K)


def _dispatch(fill, dest, h2, n_rows):
    n_tok = h2.shape[0] // SUBLANES
    tm = ROUTE_TILE
    return pl.pallas_call(
        _dispatch_kernel,
        out_shape=jax.ShapeDtypeStruct((n_rows * SUBLANES, LANES), h2.dtype),
        grid_spec=pltpu.PrefetchScalarGridSpec(
            num_scalar_prefetch=1,
            grid=(n_tok // tm,),
            in_specs=[pl.BlockSpec((1, 1, tm * TOP_K), lambda i, f: (i, 0, 0), memory_space=pltpu.SMEM),
                      pl.BlockSpec((tm * SUBLANES, LANES), lambda i, f: (i, 0))],
            out_specs=pl.BlockSpec(memory_space=pl.ANY),
            scratch_shapes=[pltpu.VMEM((SUBLANES, LANES), h2.dtype), pltpu.SemaphoreType.DMA(()),
                            pltpu.SemaphoreType.DMA(())]),
        compiler_params=_cparams(("arbitrary",)),
        name="dispatch",
    )(fill, _dest_table(dest, tm), h2)


COMBINE_TILE = 256


def _combine_kernel(dest_ref, dest_next_ref, x1_ref, gates_ref, gate2_ref, gf_ref, ys_ref, o_ref, buf_ref, sem):
    tm = x1_ref.shape[1]
    step = pl.program_id(0)
    slot = lax.rem(step, 2)

    def row_copy(table_ref, sl, t, kk):
        return pltpu.make_async_copy(_tile_row(ys_ref, table_ref[0, 0, t * TOP_K + kk]),
                                     _tile_row(buf_ref.at[sl, kk], t), sem.at[sl])

    def gather(table_ref, sl):
        def issue(t, carry):
            for kk in range(TOP_K):
                row_copy(table_ref, sl, t, kk).start(priority=kk % 2)
            return carry

        lax.fori_loop(0, tm, issue, 0, unroll=DMA_UNROLL)

    @pl.when(step == 0)
    def _():
        gather(dest_ref, slot)

    @pl.when(step + 1 < pl.num_programs(0))
    def _():
        gather(dest_next_ref, 1 - slot)

    def drain(t, carry):
        for kk in range(TOP_K):
            row_copy(dest_ref, slot, t, kk).wait()
        return carry

    lax.fori_loop(0, tm, drain, 0, unroll=DMA_UNROLL)

    gates = gates_ref[...]
    moe = gates[:, 0:1] * _load_tile_rows(buf_ref, (slot, 0), tm)
    for kk in range(1, TOP_K):
        moe = moe + gates[:, kk:kk + 1] * _load_tile_rows(buf_ref, (slot, kk), tm)
    x2 = x1_ref[0] + gate2_ref[0] * moe
    o_ref[0] = _rms(x2) * gf_ref[...]


def _combine(dest, x1, gates_t, gate2, gf, ys):
    b, l, d = x1.shape
    tm = COMBINE_TILE
    per_b = l // tm
    n_tiles = b * per_b
    table = _dest_table(dest, tm)
    table_spec = lambda nxt: pl.BlockSpec((1, 1, tm * TOP_K), lambda g: (jnp.minimum(g + nxt, n_tiles - 1), 0, 0),
                                          memory_space=pltpu.SMEM)
    return pl.pallas_call(
        _combine_kernel,
        out_shape=jax.ShapeDtypeStruct((b, l, d), F32),
        grid=(n_tiles,),
        in_specs=[table_spec(0), table_spec(1),
                  pl.BlockSpec((1, tm, d), lambda g: (g // per_b, g % per_b, 0)),
                  pl.BlockSpec((tm, TOP_K), lambda g: (g, 0)),
                  pl.BlockSpec((1, 1, d), lambda g: (g // per_b, 0, 0)),
                  _resident((1, d)),
                  pl.BlockSpec(memory_space=pl.ANY)],
        out_specs=pl.BlockSpec((1, tm, d), lambda g: (g // per_b, g % per_b, 0)),
        scratch_shapes=[pltpu.VMEM((2, TOP_K) + _tile_rows_shape(tm, d), ys.dtype),
                        pltpu.SemaphoreType.DMA((2,))],
        compiler_params=_cparams(("arbitrary",)),
        name="combine",
    )(table, table, x1, gates_t, gate2, gf, ys)


def kernel(x, c, ctx, c_ctx, norm1_g, norm2_g, ada_w, ada_b, w_in, ret_decay_fwd, ret_decay_bwd, mla_q_norm_g, mla_w_uq, mla_kv_norm_g, mla_w_ukv, w_branch_ret, w_branch_mla, w_out, router_w, router_b, exp_w_gu, exp_b_gu, exp_w_down, exp_b_down, final_norm_g):
    depth = norm1_g.shape[0]
    assert depth == 1, "single-layer block"
    b, l, d = x.shape
    qk_scale = float((MLA_QK_NOPE + MLA_QK_ROPE) ** -0.5 * np.log2(np.e))

    n_mod = b + 1
    rows = -(-n_mod // 8) * 8
    cvec = jnp.concatenate([c, c_ctx[None, :], jnp.zeros((rows - n_mod, d), F32)], axis=0)
    mod = _ada_mod(cvec, ada_w[0], ada_b[0])
    m_lat = [mod[:b, i * d:(i + 1) * d].reshape(b, 1, d) for i in range(6)]
    m_ctx = [mod[b:b + 1, i * d:(i + 1) * d] for i in range(2)]

    w_lat, w_ctx, uq, uk, uvt = _prep_weights(w_in[0], mla_w_uq[0], mla_w_ukv[0])
    tret, tmla = _rope_tables(l, qk_scale)
    g1 = norm1_g[0].reshape(1, d)
    gq = mla_q_norm_g[0].reshape(1, MLA_Q_RANK)
    gkv = mla_kv_norm_g[0].reshape(1, MLA_KV_RANK)

    rq, rk, rv, sgate, q, k_lat, vt_lat, sgr, sgm = _inproj_lat(
        x, m_lat[0], m_lat[1], g1, w_lat, uq, uk, uvt, gq, gkv, tret, tmla, qk_scale)
    rk_ctx, rv_ctx, k_ctx, vt_ctx = _inproj_ctx(ctx, m_ctx[0], m_ctx[1], g1, w_ctx, uk, uvt, gkv)

    lg = jnp.stack([jax.nn.log_sigmoid(ret_decay_fwd[0].astype(F32)),
                    jax.nn.log_sigmoid(ret_decay_bwd[0].astype(F32))])
    ret_o = _retention(lg, rq, rk, rv, rk_ctx, rv_ctx)
    att_o = _attention(q, k_lat, k_ctx, vt_lat, vt_ctx)

    wrt = jnp.pad(router_w[0], ((0, 0), (0, LANES - N_EXPERTS))).astype(BF16)
    brt = jnp.pad(router_b[0], (0, LANES - N_EXPERTS)).reshape(1, LANES)
    x1, h2, logits = _merge(x, ret_o, sgate, att_o, sgr, sgm, m_lat[2], m_lat[3], m_lat[4],
                            norm2_g[0].reshape(1, d), w_branch_ret[0].astype(BF16),
                            w_branch_mla[0].astype(BF16), w_out[0].astype(BF16), wrt, brt)

    n_tok = b * l
    idx, gates, rank, counts = _router(logits.reshape(n_tok, LANES))
    dest, block_e, n_used, fill, n_rows = _layout(idx, rank, counts, n_tok)
    xs = _dispatch(fill, dest, h2.reshape(_tile_rows_shape(n_tok, d)), n_rows)
    ys = _experts(block_e, n_used, xs, exp_w_gu[0], exp_b_gu[0], exp_w_down[0], exp_b_down[0])
    return _combine(dest, x1, gates[:TOP_K].T, m_lat[5], final_norm_g.reshape(1, d), ys)
```

```python
import functools

import numpy as np
import jax
import jax.numpy as jnp
from jax import lax
from jax.experimental import pallas as pl
from jax.experimental.pallas import tpu as pltpu

GRID_W = 64
N_RET_HEADS = 4
RET_QK_DIM = 256
RET_V_DIM = 512
N_MLA_HEADS = 8
MLA_Q_RANK = 384
MLA_KV_RANK = 256
MLA_QK_NOPE = 128
MLA_QK_ROPE = 64
MLA_V_DIM = 128
N_EXPERTS = 32
TOP_K = 4
SWIGLU_LIMIT = 7.0
SWIGLU_ALPHA = 1.702
ROPE_BASE = 10000.0
EPS = 1e-6

LANES = 128
MLA_HEAD_PAD = 2 * LANES
RET_CHUNK = 256
VMEM_LIMIT = 56 * 1024 * 1024
INPROJ_TILE = 256
MERGE_TILE = 256

F32 = jnp.float32
BF16 = jnp.bfloat16


def _cparams(sem):
    return pltpu.CompilerParams(dimension_semantics=sem, vmem_limit_bytes=VMEM_LIMIT)


def _resident(shape):
    nd = len(shape)
    return pl.BlockSpec(shape, lambda *_: (0,) * nd, pipeline_mode=pl.Buffered(1))


def _dot(a, b):
    return jnp.dot(a, b, preferred_element_type=F32)


def _dot_nt(a, b):
    return lax.dot_general(a, b, (((1,), (1,)), ((), ())), preferred_element_type=F32)


def _dot_tn(a, b):
    return lax.dot_general(a, b, (((0,), (0,)), ((), ())), preferred_element_type=F32)


SUBLANES = 8


def _tile_rows_shape(n, d):
    assert d == SUBLANES * LANES
    return (n * SUBLANES, LANES)


def _store_tile_rows(ref, lead, val):
    rows = val.shape[0]
    for s in range(SUBLANES):
        ref[lead + (pl.ds(s, rows, stride=SUBLANES), slice(None))] = val[:, s * LANES:(s + 1) * LANES]


def _load_tile_rows(ref, lead, rows):
    return jnp.concatenate([ref[lead + (pl.ds(s, rows, stride=SUBLANES), slice(None))]
                            for s in range(SUBLANES)], axis=1)


def _tile_row(ref, r):
    return ref.at[pl.ds(pl.multiple_of(r * SUBLANES, SUBLANES), SUBLANES), :]


def _rms(x):
    return x * lax.rsqrt(jnp.mean(x * x, axis=-1, keepdims=True) + EPS)


def _sigmoid(x):
    return 1.0 / (1.0 + jnp.exp(-x))


def _ada_kernel(c_ref, w_ref, b_ref, o_ref):
    c = c_ref[...]
    s = c * _sigmoid(c)
    o_ref[...] = jnp.dot(s, w_ref[...], preferred_element_type=F32,
                         precision=lax.Precision.HIGHEST) + b_ref[...]


def _ada_mod(cvec, w, b):
    rows, d = cvec.shape
    n = w.shape[1]
    tn = 1024
    return pl.pallas_call(
        _ada_kernel,
        out_shape=jax.ShapeDtypeStruct((rows, n), F32),
        grid=(n // tn,),
        in_specs=[pl.BlockSpec((rows, d), lambda j: (0, 0)),
                  pl.BlockSpec((d, tn), lambda j: (0, j)),
                  pl.BlockSpec((1, tn), lambda j: (0, j))],
        out_specs=pl.BlockSpec((rows, tn), lambda j: (0, j)),
        compiler_params=_cparams(("arbitrary",)),
        name="ada_mod",
    )(cvec, w, b.reshape(1, n))


def _rope_tables(seq_len, qk_scale):
    pos = np.arange(seq_len)
    rows = (pos // GRID_W).astype(np.float32)
    cols = (pos % GRID_W).astype(np.float32)

    def angles(p, half):
        freqs = (np.float32(ROPE_BASE) ** (-np.arange(half, dtype=np.float32) / np.float32(half))).astype(np.float32)
        return (p[:, None] * freqs[None, :]).astype(np.float32)

    a_r, a_c = angles(rows, 64), angles(cols, 64)
    ret = np.concatenate([np.cos(a_r), np.cos(a_r), np.cos(a_c), np.cos(a_c),
                          -np.sin(a_r), np.sin(a_r), -np.sin(a_c), np.sin(a_c)], axis=1)
    b_r, b_c = angles(rows, 16), angles(cols, 16)
    z32 = np.zeros((seq_len, 32), np.float32)
    z64 = np.zeros((seq_len, 64), np.float32)
    cos = np.concatenate([np.cos(b_r), np.cos(b_c), np.cos(b_r), np.cos(b_c), z64], axis=1)
    sin_up = np.concatenate([-np.sin(b_r), -np.sin(b_c), z32, z64], axis=1)
    sin_dn = np.concatenate([z32, np.sin(b_r), np.sin(b_c), z64], axis=1)
    mla_k = np.concatenate([cos, sin_up, sin_dn], axis=1)
    mla = np.concatenate([mla_k, mla_k * np.float32(qk_scale)], axis=1)
    return jnp.asarray(ret, F32), jnp.asarray(mla, F32)


def _rope_ret(t, tab, parity):
    cos = tab[:, parity * LANES:(parity + 1) * LANES]
    sin = tab[:, (2 + parity) * LANES:(3 + parity) * LANES]
    return t * cos + pltpu.roll(t, 64, 1) * sin


def _rope_mla(t, tab, base):
    cos = tab[:, base:base + LANES]
    sin_up = tab[:, base + LANES:base + 2 * LANES]
    sin_dn = tab[:, base + 2 * LANES:base + 3 * LANES]
    return t * cos + pltpu.roll(t, 96, 1) * sin_up + pltpu.roll(t, 32, 1) * sin_dn


_C_Q, _C_K, _C_V, _C_G = 0, 1024, 2048, 4096
_C_CQ, _C_CKV, _C_KR, _C_GR, _C_GM, _C_END = 6144, 6528, 6784, 6912, 7936, 8960
_X_K, _X_V, _X_CKV, _X_KR, _X_END = 0, 1024, 3072, 3328, 3456


def _norm_mod(x, g, shift, scale):
    return (_rms(x) * g) * (1.0 + scale) + shift


def _mla_kv(ckv_acc, kr, gkv_ref, wuk_ref, wuvt_ref, k_ref, vt_ref):
    ckvn = (_rms(ckv_acc) * gkv_ref[...]).astype(BF16)
    kn = _dot(ckvn, wuk_ref[...])
    krb = kr.astype(BF16)
    for hh in range(N_MLA_HEADS):
        k_ref[0, :, hh * MLA_HEAD_PAD:hh * MLA_HEAD_PAD + LANES] = kn[:, hh * LANES:(hh + 1) * LANES].astype(BF16)
        k_ref[0, :, hh * MLA_HEAD_PAD + LANES:(hh + 1) * MLA_HEAD_PAD] = krb
    vt_ref[0] = _dot_nt(wuvt_ref[...], ckvn).astype(BF16)


def _inproj_lat_kernel(qk_scale, x_ref, shift_ref, scale_ref, g_ref, w_ref, wuq_ref, wuk_ref, wuvt_ref, gq_ref,
                       gkv_ref, tret_ref, tmla_ref,
                       rq_ref, rk_ref, rv_ref, sgate_ref, q_ref, k_ref, vt_ref, sgr_ref, sgm_ref):
    h = _norm_mod(x_ref[0], g_ref[...], shift_ref[0], scale_ref[0]).astype(BF16)
    tret = tret_ref[...]
    tmla = tmla_ref[...]

    def proj(c0, n):
        return _dot(h, w_ref[:, c0:c0 + n])

    for base, out in ((_C_Q, rq_ref), (_C_K, rk_ref)):
        for j in range(2):
            acc = proj(base + j * 512, 512)
            for blk in range(4):
                t = acc[:, blk * LANES:(blk + 1) * LANES]
                col = j * 512 + blk * LANES
                out[0, :, col:col + LANES] = _rope_ret(t, tret, blk % 2).astype(BF16)
    for j in range(4):
        rv_ref[0, :, j * 512:(j + 1) * 512] = proj(_C_V + j * 512, 512).astype(BF16)
    for j in range(4):
        a = proj(_C_G + j * 512, 512)
        sgate_ref[0, :, j * 512:(j + 1) * 512] = (a * _sigmoid(a)).astype(BF16)
    for j in range(2):
        a = proj(_C_GR + j * 512, 512)
        sgr_ref[0, :, j * 512:(j + 1) * 512] = _sigmoid(a).astype(BF16)
    for j in range(2):
        a = proj(_C_GM + j * 512, 512)
        sgm_ref[0, :, j * 512:(j + 1) * 512] = _sigmoid(a).astype(BF16)

    cqn = (_rms(proj(_C_CQ, MLA_Q_RANK)) * gq_ref[...]).astype(BF16)
    for j in range(4):
        acc = _dot(cqn, wuq_ref[:, j * 512:(j + 1) * 512])
        for blk in range(4):
            t = acc[:, blk * LANES:(blk + 1) * LANES]
            col = j * 512 + blk * LANES
            if blk % 2 == 0:
                q_ref[0, :, col:col + LANES] = (t * qk_scale).astype(BF16)
            else:
                q_ref[0, :, col:col + LANES] = _rope_mla(t, tmla, 3 * LANES).astype(BF16)
    kr = _rope_mla(proj(_C_KR, LANES), tmla, 0)
    _mla_kv(proj(_C_CKV, MLA_KV_RANK), kr, gkv_ref, wuk_ref, wuvt_ref, k_ref, vt_ref)


def _inproj_ctx_kernel(x_ref, shift_ref, scale_ref, g_ref, w_ref, wuk_ref, wuvt_ref, gkv_ref,
                       rk_ref, rv_ref, k_ref, vt_ref):
    h = _norm_mod(x_ref[0], g_ref[...], shift_ref[...], scale_ref[...]).astype(BF16)

    def proj(c0, n):
        return _dot(h, w_ref[:, c0:c0 + n])

    for j in range(2):
        rk_ref[0, :, j * 512:(j + 1) * 512] = proj(_X_K + j * 512, 512).astype(BF16)
    for j in range(4):
        rv_ref[0, :, j * 512:(j + 1) * 512] = proj(_X_V + j * 512, 512).astype(BF16)
    _mla_kv(proj(_X_CKV, MLA_KV_RANK), proj(_X_KR, LANES), gkv_ref, wuk_ref, wuvt_ref, k_ref, vt_ref)


def _prep_weights(w_in, w_uq, w_ukv):
    nq = N_RET_HEADS * RET_QK_DIM
    nv = N_RET_HEADS * RET_V_DIM
    d = w_in.shape[0]
    sizes = (nq, nq, nv, nv, MLA_Q_RANK, MLA_KV_RANK, MLA_QK_ROPE, d, d)
    offs = np.cumsum((0,) + sizes)
    wq, wk, wv, wg, wcq, wckv, wkr, wgr, wgm = [w_in[:, offs[i]:offs[i + 1]] for i in range(9)]
    wk = wk * (RET_QK_DIM ** -0.5)
    perm = np.concatenate([np.arange(0, 16), np.arange(32, 48), np.arange(16, 32), np.arange(48, 64)])
    wkr = jnp.pad(wkr[:, perm], ((0, 0), (0, LANES - MLA_QK_ROPE)))
    w_lat = jnp.concatenate([wq, wk, wv, wg, wcq, wckv, wkr, wgr, wgm], axis=1).astype(BF16)
    w_ctx = jnp.concatenate([wk, wv, wckv, wkr], axis=1).astype(BF16)
    uq = w_uq.reshape(MLA_Q_RANK, N_MLA_HEADS, MLA_QK_NOPE + MLA_QK_ROPE)
    uq = jnp.concatenate([uq[:, :, :MLA_QK_NOPE], uq[:, :, MLA_QK_NOPE:][:, :, perm],
                          jnp.zeros((MLA_Q_RANK, N_MLA_HEADS, LANES - MLA_QK_ROPE), w_uq.dtype)], axis=2)
    uq = uq.reshape(MLA_Q_RANK, N_MLA_HEADS * MLA_HEAD_PAD).astype(BF16)
    ukv = w_ukv.reshape(MLA_KV_RANK, N_MLA_HEADS, MLA_QK_NOPE + MLA_V_DIM)
    uk = ukv[:, :, :MLA_QK_NOPE].reshape(MLA_KV_RANK, -1).astype(BF16)
    uvt = ukv[:, :, MLA_QK_NOPE:].reshape(MLA_KV_RANK, -1).T.astype(BF16)
    return w_lat, w_ctx, uq, uk, uvt


def _tok_spec(tm, n):
    return pl.BlockSpec((1, tm, n), lambda i, j: (i, j, 0))


def _vt_spec(tm):
    return pl.BlockSpec((1, N_MLA_HEADS * MLA_V_DIM, tm), lambda i, j: (i, 0, j))


def _inproj_lat(x, shift, scale, g, w_lat, uq, uk, uvt, gq, gkv, tret, tmla, qk_scale):
    b, l, d = x.shape
    tm = INPROJ_TILE
    vec = pl.BlockSpec((1, 1, d), lambda i, j: (i, 0, 0))
    out_w = (1024, 1024, 2048, 2048, 2048, 2048, None, 1024, 1024)
    nvt = N_MLA_HEADS * MLA_V_DIM
    return pl.pallas_call(
        functools.partial(_inproj_lat_kernel, qk_scale),
        out_shape=[jax.ShapeDtypeStruct((b, nvt, l) if n is None else (b, l, n), BF16) for n in out_w],
        grid=(b, l // tm),
        in_specs=[_tok_spec(tm, d), vec, vec, _resident((1, d)), _resident(w_lat.shape), _resident(uq.shape),
                  _resident(uk.shape), _resident(uvt.shape), _resident((1, MLA_Q_RANK)),
                  _resident((1, MLA_KV_RANK)),
                  pl.BlockSpec((tm, tret.shape[1]), lambda i, j: (j, 0)),
                  pl.BlockSpec((tm, tmla.shape[1]), lambda i, j: (j, 0))],
        out_specs=[_vt_spec(tm) if n is None else _tok_spec(tm, n) for n in out_w],
        compiler_params=_cparams(("arbitrary", "arbitrary")),
        name="inproj_lat",
    )(x, shift, scale, g, w_lat, uq, uk, uvt, gq, gkv, tret, tmla)


def _inproj_ctx(ctx, shift, scale, g, w_ctx, uk, uvt, gkv):
    b, l, d = ctx.shape
    tm = 256
    out_w = (1024, 2048, 2048, None)
    nvt = N_MLA_HEADS * MLA_V_DIM
    return pl.pallas_call(
        _inproj_ctx_kernel,
        out_shape=[jax.ShapeDtypeStruct((b, nvt, l) if n is None else (b, l, n), BF16) for n in out_w],
        grid=(b, l // tm),
        in_specs=[_tok_spec(tm, d), _resident((1, d)), _resident((1, d)), _resident((1, d)),
                  _resident(w_ctx.shape), _resident(uk.shape), _resident(uvt.shape),
                  _resident((1, MLA_KV_RANK))],
        out_specs=[_vt_spec(tm) if n is None else _tok_spec(tm, n) for n in out_w],
        compiler_params=_cparams(("arbitrary", "arbitrary")),
        name="inproj_ctx",
    )(ctx, shift, scale, g, w_ctx, uk, uvt, gkv)


def _retention_kernel(n_chunks, n_ctx_chunks, lg_ref, q_ref, k_ref, v_ref, kc_ref, vc_ref, o_ref,
                      ob_ref, sf_ref, sb_ref, dmat_ref, dec_ref):
    c = RET_CHUNK
    head = pl.program_id(1)
    lgf = lg_ref[0, head]
    lgb = lg_ref[1, head]
    ri = lax.broadcasted_iota(jnp.int32, (c, c), 0).astype(F32)
    ci = lax.broadcasted_iota(jnp.int32, (c, c), 1).astype(F32)
    diff = ri - ci
    dmat_ref[...] = jnp.where(diff >= 0, jnp.exp(lgf * jnp.maximum(diff, 0.0)),
                              jnp.exp(lgb * jnp.maximum(-diff, 0.0)))
    rk = lax.broadcasted_iota(jnp.int32, (c, RET_QK_DIM), 0).astype(F32)
    dec_ref[0] = jnp.exp(lgf * (rk + 1.0))
    dec_ref[1] = jnp.exp(lgf * (c - 1.0 - rk))
    dec_ref[2] = jnp.exp(lgb * (c - rk))
    dec_ref[3] = jnp.exp(lgb * rk)
    cdf = jnp.exp(jnp.full((1, RET_V_DIM), lgf * c, F32))
    cdb = jnp.exp(jnp.full((1, RET_V_DIM), lgb * c, F32))

    def scaled(t, which):
        return (t.astype(F32) * dec_ref[which]).astype(BF16)

    sf_ref[...] = jnp.zeros_like(sf_ref)
    sb_ref[...] = jnp.zeros_like(sb_ref)
    for n in range(n_ctx_chunks):
        sl = slice(n * c, (n + 1) * c)
        sf_ref[...] = sf_ref[...] * cdf + _dot_tn(scaled(kc_ref[0, sl, :], 1), vc_ref[0, sl, :])
    for n in reversed(range(n_ctx_chunks)):
        sl = slice(n * c, (n + 1) * c)
        sb_ref[...] = sb_ref[...] * cdb + _dot_tn(scaled(kc_ref[0, sl, :], 3), vc_ref[0, sl, :])

    def step(i, finish):
        rb = pl.multiple_of((n_chunks - 1 - i) * c, c)
        rf = pl.multiple_of(i * c, c)
        qf = q_ref[0, pl.ds(rf, c), :]
        kf = k_ref[0, pl.ds(rf, c), :]
        vf = v_ref[0, pl.ds(rf, c), :]
        qb = q_ref[0, pl.ds(rb, c), :]
        kb = k_ref[0, pl.ds(rb, c), :]
        vb = v_ref[0, pl.ds(rb, c), :]
        s = _dot_nt(qf, kf)
        o_b = _dot(scaled(qb, 2), sb_ref[...].astype(BF16))
        sb_new = _dot_tn(scaled(kb, 3), vb)
        o_f = _dot(scaled(qf, 0), sf_ref[...].astype(BF16))
        sf_new = _dot_tn(scaled(kf, 1), vf)
        o_f = o_f + _dot((s * dmat_ref[...]).astype(BF16), vf)
        sb_ref[...] = sb_ref[...] * cdb + sb_new
        sf_ref[...] = sf_ref[...] * cdf + sf_new
        if finish:
            o_ref[0, pl.ds(rb, c), :] = _rms(o_b + ob_ref[pl.ds(rb, c), :]).astype(BF16)
            o_ref[0, pl.ds(rf, c), :] = _rms(o_f + ob_ref[pl.ds(rf, c), :]).astype(BF16)
        else:
            ob_ref[pl.ds(rb, c), :] = o_b
            ob_ref[pl.ds(rf, c), :] = o_f

    half = n_chunks // 2
    lax.fori_loop(0, half, lambda i, carry: (step(i, False), carry)[1], 0)
    lax.fori_loop(half, n_chunks, lambda i, carry: (step(i, True), carry)[1], 0)


def _retention(lg, rq, rk, rv, rk_ctx, rv_ctx):
    b, l, _ = rq.shape
    lc = rk_ctx.shape[1]
    c = RET_CHUNK
    assert l % (2 * c) == 0 and lc % c == 0
    qk = lambda n: pl.BlockSpec((1, n, RET_QK_DIM), lambda i, h: (i, 0, h))
    vv = lambda n: pl.BlockSpec((1, n, RET_V_DIM), lambda i, h: (i, 0, h))
    return pl.pallas_call(
        functools.partial(_retention_kernel, l // c, lc // c),
        out_shape=jax.ShapeDtypeStruct((b, l, N_RET_HEADS * RET_V_DIM), BF16),
        grid=(b, N_RET_HEADS),
        in_specs=[pl.BlockSpec(memory_space=pltpu.SMEM), qk(l), qk(l), vv(l), qk(lc), vv(lc)],
        out_specs=vv(l),
        scratch_shapes=[pltpu.VMEM((l, RET_V_DIM), F32),
                        pltpu.VMEM((RET_QK_DIM, RET_V_DIM), F32),
                        pltpu.VMEM((RET_QK_DIM, RET_V_DIM), F32),
                        pltpu.VMEM((c, c), F32),
                        pltpu.VMEM((4, c, RET_QK_DIM), F32)],
        compiler_params=_cparams(("arbitrary", "arbitrary")),
        name="retention",
    )(lg, rq, rk, rv, rk_ctx, rv_ctx)


ATTN_CHAIN = 256
ATTN_CHAINS = 8


ATTN_KEY_BLOCK = 1024


def _attention_kernel(q_ref, kl_ref, kc_ref, vtl_ref, vtc_ref, o_ref):
    l = kl_ref.shape[1]
    lc = kc_ref.shape[1]
    blocks = [(kc_ref, vtc_ref, 0, lc)]
    blocks += [(kl_ref, vtl_ref, r0, ATTN_KEY_BLOCK) for r0 in range(0, l, ATTN_KEY_BLOCK)]
    chains = range(ATTN_CHAINS)
    rows = [slice(ch * ATTN_CHAIN, (ch + 1) * ATTN_CHAIN) for ch in chains]
    q = [q_ref[0, rows[ch], :] for ch in chains]

    def scores(ch, j):
        k_ref, _, r0, n = blocks[j]
        return _dot_nt(k_ref[0, r0:r0 + n, :], q[ch])

    s = [scores(ch, 0) for ch in chains]
    m = [None] * ATTN_CHAINS
    denom = [None] * ATTN_CHAINS
    acc = [None] * ATTN_CHAINS
    for j in range(len(blocks)):
        _, vt_ref, r0, n = blocks[j]
        for ch in chains:
            s_cur = s[ch]
            if j + 1 < len(blocks):
                s[ch] = scores(ch, j + 1)
            m_blk = jnp.max(s_cur, axis=0, keepdims=True)
            if j == 0:
                m[ch] = m_blk
                p = jnp.exp2(s_cur - m_blk)
                denom[ch] = jnp.sum(p, axis=0, keepdims=True)
                acc[ch] = _dot(vt_ref[0, :, r0:r0 + n], p.astype(BF16))
            else:
                m_new = jnp.maximum(m[ch], m_blk)
                alpha = jnp.exp2(m[ch] - m_new)
                p = jnp.exp2(s_cur - m_new)
                denom[ch] = denom[ch] * alpha + jnp.sum(p, axis=0, keepdims=True)
                acc[ch] = acc[ch] * alpha + _dot(vt_ref[0, :, r0:r0 + n], p.astype(BF16))
                m[ch] = m_new
    for ch in chains:
        o_ref[0, rows[ch], :] = (acc[ch] / denom[ch]).T.astype(BF16)


def _attention(q, k_lat, k_ctx, vt_lat, vt_ctx):
    b, l, _ = q.shape
    lc = k_ctx.shape[1]
    tq = ATTN_CHAIN * ATTN_CHAINS
    return pl.pallas_call(
        _attention_kernel,
        out_shape=jax.ShapeDtypeStruct((b, l, N_MLA_HEADS * MLA_V_DIM), BF16),
        grid=(b, N_MLA_HEADS, l // tq),
        in_specs=[pl.BlockSpec((1, tq, MLA_HEAD_PAD), lambda i, h, j: (i, j, h)),
                  pl.BlockSpec((1, l, MLA_HEAD_PAD), lambda i, h, j: (i, 0, h)),
                  pl.BlockSpec((1, lc, MLA_HEAD_PAD), lambda i, h, j: (i, 0, h)),
                  pl.BlockSpec((1, MLA_V_DIM, l), lambda i, h, j: (i, h, 0)),
                  pl.BlockSpec((1, MLA_V_DIM, lc), lambda i, h, j: (i, h, 0))],
        out_specs=pl.BlockSpec((1, tq, MLA_V_DIM), lambda i, h, j: (i, j, h)),
        compiler_params=_cparams(("arbitrary", "arbitrary", "arbitrary")),
        name="attention",
    )(q, k_lat, k_ctx, vt_lat, vt_ctx)


def _merge_kernel(x_ref, ret_ref, sgate_ref, att_ref, sgr_ref, sgm_ref, gate1_ref, shift2_ref, scale2_ref,
                  g2_ref, wr_ref, wm_ref, wo_ref, wrt_ref, brt_ref, x1_ref, h2_ref, logit_ref):
    r = (ret_ref[0].astype(F32) * sgate_ref[0].astype(F32)).astype(BF16)
    merged = (sgr_ref[0].astype(F32) * _dot(r, wr_ref[...])
              + sgm_ref[0].astype(F32) * _dot(att_ref[0], wm_ref[...]))
    y = _dot(merged.astype(BF16), wo_ref[...])
    x1 = x_ref[0] + gate1_ref[0] * y
    x1_ref[0] = x1
    h2 = _norm_mod(x1, g2_ref[...], shift2_ref[0], scale2_ref[0])
    _store_tile_rows(h2_ref, (0,), h2)
    logit_ref[0] = _dot(h2.astype(BF16), wrt_ref[...]) + brt_ref[...]


def _merge(x, ret_o, sgate, att_o, sgr, sgm, gate1, shift2, scale2, g2, wr, wm, wo, wrt, brt):
    b, l, d = x.shape
    tm = MERGE_TILE
    tok = lambda n: pl.BlockSpec((1, tm, n), lambda i, j: (i, j, 0))
    vec = pl.BlockSpec((1, 1, d), lambda i, j: (i, 0, 0))
    return pl.pallas_call(
        _merge_kernel,
        out_shape=[jax.ShapeDtypeStruct((b, l, d), F32),
                   jax.ShapeDtypeStruct((b,) + _tile_rows_shape(l, d), F32),
                   jax.ShapeDtypeStruct((b, l, LANES), F32)],
        grid=(b, l // tm),
        in_specs=[tok(d), tok(ret_o.shape[2]), tok(sgate.shape[2]), tok(att_o.shape[2]), tok(d), tok(d),
                  vec, vec, vec, _resident((1, d)), _resident(wr.shape), _resident(wm.shape),
                  _resident(wo.shape), _resident(wrt.shape), _resident(brt.shape)],
        out_specs=[tok(d), pl.BlockSpec((1,) + _tile_rows_shape(tm, d), lambda i, j: (i, j, 0)), tok(LANES)],
        compiler_params=_cparams(("arbitrary", "arbitrary")),
        name="merge",
    )(x, ret_o, sgate, att_o, sgr, sgm, gate1, shift2, scale2, g2, wr, wm, wo, wrt, brt)


EXPERT_BLOCK = 512


def _expert_kernel(be_ref, nb_ref, x_ref, wgu_ref, bgu_ref, wd_ref, bd_ref, y_ref, wgu_bf, wd_bf):
    step = pl.program_id(0)

    @pl.when(step >= nb_ref[0])
    def _():
        y_ref[...] = jnp.zeros_like(y_ref)

    @pl.when((step < nb_ref[0]) & ((step == 0) | (be_ref[step] != be_ref[jnp.maximum(step - 1, 0)])))
    def _():
        cw = 4 * LANES
        for c0 in range(0, wgu_bf.shape[1], cw):
            wgu_bf[:, c0:c0 + cw] = wgu_ref[0, :, c0:c0 + cw].astype(BF16)
        for c0 in range(0, wd_bf.shape[1], cw):
            wd_bf[:, c0:c0 + cw] = wd_ref[0, :, c0:c0 + cw].astype(BF16)

    @pl.when(step < nb_ref[0])
    def _():
        f = wd_bf.shape[0]
        x = _load_tile_rows(x_ref, (), EXPERT_BLOCK).astype(BF16)
        gate = _dot(x, wgu_bf[:, :f]) + bgu_ref[0, :, :f]
        up = _dot(x, wgu_bf[:, f:]) + bgu_ref[0, :, f:]
        gate = jnp.minimum(gate, SWIGLU_LIMIT)
        up = jnp.clip(up, -SWIGLU_LIMIT, SWIGLU_LIMIT)
        glu = gate * _sigmoid(SWIGLU_ALPHA * gate)
        act = ((up + 1.0) * glu).astype(BF16)
        _store_tile_rows(y_ref, (), _dot(act, wd_bf[...]) + bd_ref[0])


def _experts(block_e, n_used, xs, w_gu, b_gu, w_down, b_down):
    n_rows = xs.shape[0] // SUBLANES
    e, d, f2 = w_gu.shape
    f = f2 // 2
    bm = EXPERT_BLOCK
    blk = _tile_rows_shape(bm, d)
    return pl.pallas_call(
        _expert_kernel,
        out_shape=jax.ShapeDtypeStruct(xs.shape, F32),
        grid_spec=pltpu.PrefetchScalarGridSpec(
            num_scalar_prefetch=2,
            grid=(n_rows // bm,),
            in_specs=[pl.BlockSpec(blk, lambda i, be, nb: (jnp.minimum(i, nb[0] - 1), 0)),
                      pl.BlockSpec((1, d, f2), lambda i, be, nb: (be[i], 0, 0)),
                      pl.BlockSpec((1, 1, f2), lambda i, be, nb: (be[i], 0, 0)),
                      pl.BlockSpec((1, f, d), lambda i, be, nb: (be[i], 0, 0)),
                      pl.BlockSpec((1, 1, d), lambda i, be, nb: (be[i], 0, 0))],
            out_specs=pl.BlockSpec(blk, lambda i, be, nb: (i, 0)),
            scratch_shapes=[pltpu.VMEM((d, f2), BF16), pltpu.VMEM((f, d), BF16)]),
        compiler_params=_cparams(("arbitrary",)),
        name="experts",
    )(block_e, n_used, xs, w_gu, b_gu.reshape(e, 1, f2), w_down, b_down.reshape(e, 1, d))


ROUTE_TILE = 512
ROUTE_ROWS = 8
DMA_UNROLL = 8


def _router_kernel(logit_ref, idx_ref, gate_ref, rank_ref, cnt_ref, base_ref, tri_ref):
    tm = logit_ref.shape[0]
    step = pl.program_id(0)

    @pl.when(step == 0)
    def _():
        base_ref[...] = jnp.zeros_like(base_ref)
        r = lax.broadcasted_iota(jnp.int32, (tm, tm), 0)
        c = lax.broadcasted_iota(jnp.int32, (tm, tm), 1)
        tri_ref[...] = jnp.where(r < c, 1.0, 0.0).astype(BF16)

    v = logit_ref[...].T[:N_EXPERTS, :]
    eid = lax.broadcasted_iota(jnp.int32, (N_EXPERTS, tm), 0)
    onehot = jnp.zeros((N_EXPERTS, tm), F32)
    vals, sels = [], []
    for kk in range(TOP_K):
        mx = jnp.max(v, axis=0, keepdims=True)
        ik = jnp.min(jnp.where(v == mx, eid, N_EXPERTS), axis=0, keepdims=True)
        sel = eid == ik
        idx_ref[kk:kk + 1, :] = ik
        vals.append(mx)
        sels.append(sel)
        onehot = onehot + jnp.where(sel, 1.0, 0.0)
        v = jnp.where(sel, -jnp.inf, v)
    ex = [jnp.exp(val - vals[0]) for val in vals]
    tot = ex[0] + ex[1] + ex[2] + ex[3]
    for kk in range(TOP_K):
        gate_ref[kk:kk + 1, :] = ex[kk] / tot
    rank_e = base_ref[:, 0:1] + _dot(onehot.astype(BF16), tri_ref[...])
    for kk in range(TOP_K):
        rank_ref[kk:kk + 1, :] = jnp.sum(jnp.where(sels[kk], rank_e, 0.0), axis=0, keepdims=True).astype(jnp.int32)
    zero_i = jnp.zeros((ROUTE_ROWS - TOP_K, tm), jnp.int32)
    idx_ref[TOP_K:, :] = zero_i
    rank_ref[TOP_K:, :] = zero_i
    gate_ref[TOP_K:, :] = jnp.zeros((ROUTE_ROWS - TOP_K, tm), F32)
    base_ref[...] = base_ref[...] + jnp.sum(onehot, axis=1, keepdims=True)
    cnt_ref[...] = base_ref[...]


def _router(logits):
    n_tok = logits.shape[0]
    tm = ROUTE_TILE
    tab = lambda: pl.BlockSpec((ROUTE_ROWS, tm), lambda i: (0, i))
    return pl.pallas_call(
        _router_kernel,
        out_shape=[jax.ShapeDtypeStruct((ROUTE_ROWS, n_tok), jnp.int32),
                   jax.ShapeDtypeStruct((ROUTE_ROWS, n_tok), F32),
                   jax.ShapeDtypeStruct((ROUTE_ROWS, n_tok), jnp.int32),
                   jax.ShapeDtypeStruct((N_EXPERTS, LANES), F32)],
        grid=(n_tok // tm,),
        in_specs=[pl.BlockSpec((tm, LANES), lambda i: (i, 0))],
        out_specs=[tab(), tab(), tab(), pl.BlockSpec((N_EXPERTS, LANES), lambda i: (0, 0))],
        scratch_shapes=[pltpu.VMEM((N_EXPERTS, LANES), F32), pltpu.VMEM((tm, tm), BF16)],
        compiler_params=_cparams(("arbitrary",)),
        name="router",
    )(logits)


def _layout(idx, rank, counts, n_tok):
    bm = EXPERT_BLOCK
    counts = counts[:, 0].astype(jnp.int32)
    padded = (counts + bm - 1) // bm * bm
    pad_end = jnp.cumsum(padded)
    pad_start = pad_end - padded
    n_rows = n_tok * TOP_K + N_EXPERTS * bm
    n_blocks = n_rows // bm
    blk_start = jnp.arange(n_blocks, dtype=jnp.int32) * bm
    block_e = jnp.minimum(jnp.sum(blk_start[:, None] >= pad_end[None, :], axis=1), N_EXPERTS - 1).astype(jnp.int32)
    n_used = (pad_end[-1] // bm).astype(jnp.int32).reshape(1)
    start_of = jnp.zeros_like(idx)
    for e in range(N_EXPERTS):
        start_of = jnp.where(idx == e, pad_start[e], start_of)
    dest = start_of + rank
    fill = jnp.stack([pad_start + counts, pad_end]).astype(jnp.int32)
    return dest, block_e, n_used, fill, n_rows


def _dispatch_kernel(fill_ref, dest_ref, h_ref, xs_ref, zero_ref, sem, zsem):
    tm = dest_ref.shape[2] // TOP_K
    step = pl.program_id(0)
    last = pl.num_programs(0) - 1
    slot = lax.rem(step, 2)
    tok0 = step * tm

    def row_copy(t, kk, sl, dst=None):
        dst = dest_ref[0, 0, t * TOP_K + kk] if dst is None else dst
        return pltpu.make_async_copy(_tile_row(h_ref, tok0 + t), _tile_row(xs_ref, dst), sem.at[sl])

    def issue(g, carry):
        t0 = g * DMA_UNROLL
        dst = [[dest_ref[0, 0, (t0 + u) * TOP_K + kk] for kk in range(TOP_K)] for u in range(DMA_UNROLL)]
        for u in range(DMA_UNROLL):
            for kk in range(TOP_K):
                row_copy(t0 + u, kk, slot, dst[u][kk]).start(priority=kk % 2)
        return carry

    lax.fori_loop(0, tm // DMA_UNROLL, issue, 0)

    def drain(sl):
        def body(t, carry):
            for kk in range(TOP_K):
                row_copy(t, kk, sl).wait()
            return carry

        lax.fori_loop(0, tm, body, 0, unroll=DMA_UNROLL)

    @pl.when(step > 0)
    def _():
        drain(1 - slot)

    @pl.when(step == last)
    def _():
        drain(slot)

    @pl.when(step == last)
    def _():
        zero_ref[...] = jnp.zeros_like(zero_ref)

        def zero_copy(r):
            return pltpu.make_async_copy(zero_ref, _tile_row(xs_ref, r), zsem)

        def per_expert(e, carry):
            lo = fill_ref[0, e]
            hi = fill_ref[1, e]
            lax.fori_loop(lo, hi, lambda r, c: (zero_copy(r).start(), c)[1], 0)
            lax.fori_loop(lo, hi, lambda r, c: (zero_copy(r).wait(), c)[1], 0)
            return carry

        lax.fori_loop(0, N_EXPERTS, per_expert, 0)


def _dest_table(dest, tm):
    n_tok = dest.shape[1]
    return dest[:TOP_K].T.reshape(n_tok // tm, 1, tm * TOP_K)


def _dispatch(fill, dest, h2, n_rows):
    n_tok = h2.shape[0] // SUBLANES
    tm = ROUTE_TILE
    return pl.pallas_call(
        _dispatch_kernel,
        out_shape=jax.ShapeDtypeStruct((n_rows * SUBLANES, LANES), h2.dtype),
        grid_spec=pltpu.PrefetchScalarGridSpec(
            num_scalar_prefetch=1,
            grid=(n_tok // tm,),
            in_specs=[pl.BlockSpec((1, 1, tm * TOP_K), lambda i, f: (i, 0, 0), memory_space=pltpu.SMEM),
                      pl.BlockSpec(memory_space=pl.ANY)],
            out_specs=pl.BlockSpec(memory_space=pl.ANY),
            scratch_shapes=[pltpu.VMEM((SUBLANES, LANES), h2.dtype), pltpu.SemaphoreType.DMA((2,)),
                            pltpu.SemaphoreType.DMA(())]),
        compiler_params=_cparams(("arbitrary",)),
        name="dispatch",
    )(fill, _dest_table(dest, tm), h2)


COMBINE_TILE = 256


def _combine_kernel(dest_ref, dest_next_ref, x1_ref, gates_ref, gate2_ref, gf_ref, ys_ref, o_ref, buf_ref, sem):
    tm = x1_ref.shape[1]
    step = pl.program_id(0)
    slot = lax.rem(step, 2)

    def row_copy(table_ref, sl, t, kk, src=None):
        src = table_ref[0, 0, t * TOP_K + kk] if src is None else src
        return pltpu.make_async_copy(_tile_row(ys_ref, src), _tile_row(buf_ref.at[sl, kk], t), sem.at[sl])

    def gather(table_ref, sl):
        def issue(g, carry):
            t0 = g * DMA_UNROLL
            src = [[table_ref[0, 0, (t0 + u) * TOP_K + kk] for kk in range(TOP_K)] for u in range(DMA_UNROLL)]
            for u in range(DMA_UNROLL):
                for kk in range(TOP_K):
                    row_copy(table_ref, sl, t0 + u, kk, src[u][kk]).start(priority=kk % 2)
            return carry

        lax.fori_loop(0, tm // DMA_UNROLL, issue, 0)

    @pl.when(step == 0)
    def _():
        gather(dest_ref, slot)

    @pl.when(step + 1 < pl.num_programs(0))
    def _():
        gather(dest_next_ref, 1 - slot)

    def drain(t, carry):
        for kk in range(TOP_K):
            row_copy(dest_ref, slot, t, kk).wait()
        return carry

    lax.fori_loop(0, tm, drain, 0, unroll=DMA_UNROLL)

    gates = gates_ref[...]
    moe = gates[:, 0:1] * _load_tile_rows(buf_ref, (slot, 0), tm)
    for kk in range(1, TOP_K):
        moe = moe + gates[:, kk:kk + 1] * _load_tile_rows(buf_ref, (slot, kk), tm)
    x2 = x1_ref[0] + gate2_ref[0] * moe
    o_ref[0] = _rms(x2) * gf_ref[...]


def _combine(dest, x1, gates_t, gate2, gf, ys):
    b, l, d = x1.shape
    tm = COMBINE_TILE
    per_b = l // tm
    n_tiles = b * per_b
    table = _dest_table(dest, tm)
    table_spec = lambda nxt: pl.BlockSpec((1, 1, tm * TOP_K), lambda g: (jnp.minimum(g + nxt, n_tiles - 1), 0, 0),
                                          memory_space=pltpu.SMEM)
    return pl.pallas_call(
        _combine_kernel,
        out_shape=jax.ShapeDtypeStruct((b, l, d), F32),
        grid=(n_tiles,),
        in_specs=[table_spec(0), table_spec(1),
                  pl.BlockSpec((1, tm, d), lambda g: (g // per_b, g % per_b, 0)),
                  pl.BlockSpec((tm, TOP_K), lambda g: (g, 0)),
                  pl.BlockSpec((1, 1, d), lambda g: (g // per_b, 0, 0)),
                  _resident((1, d)),
                  pl.BlockSpec(memory_space=pl.ANY)],
        out_specs=pl.BlockSpec((1, tm, d), lambda g: (g // per_b, g % per_b, 0)),
        scratch_shapes=[pltpu.VMEM((2, TOP_K) + _tile_rows_shape(tm, d), ys.dtype),
                        pltpu.SemaphoreType.DMA((2,))],
        compiler_params=_cparams(("arbitrary",)),
        name="combine",
    )(table, table, x1, gates_t, gate2, gf, ys)


def kernel(x, c, ctx, c_ctx, norm1_g, norm2_g, ada_w, ada_b, w_in, ret_decay_fwd, ret_decay_bwd, mla_q_norm_g, mla_w_uq, mla_kv_norm_g, mla_w_ukv, w_branch_ret, w_branch_mla, w_out, router_w, router_b, exp_w_gu, exp_b_gu, exp_w_down, exp_b_down, final_norm_g):
    depth = norm1_g.shape[0]
    assert depth == 1, "single-layer block"
    b, l, d = x.shape
    qk_scale = float((MLA_QK_NOPE + MLA_QK_ROPE) ** -0.5 * np.log2(np.e))

    n_mod = b + 1
    rows = -(-n_mod // 8) * 8
    cvec = jnp.concatenate([c, c_ctx[None, :], jnp.zeros((rows - n_mod, d), F32)], axis=0)
    mod = _ada_mod(cvec, ada_w[0], ada_b[0])
    m_lat = [mod[:b, i * d:(i + 1) * d].reshape(b, 1, d) for i in range(6)]
    m_ctx = [mod[b:b + 1, i * d:(i + 1) * d] for i in range(2)]

    w_lat, w_ctx, uq, uk, uvt = _prep_weights(w_in[0], mla_w_uq[0], mla_w_ukv[0])
    tret, tmla = _rope_tables(l, qk_scale)
    g1 = norm1_g[0].reshape(1, d)
    gq = mla_q_norm_g[0].reshape(1, MLA_Q_RANK)
    gkv = mla_kv_norm_g[0].reshape(1, MLA_KV_RANK)

    rq, rk, rv, sgate, q, k_lat, vt_lat, sgr, sgm = _inproj_lat(
        x, m_lat[0], m_lat[1], g1, w_lat, uq, uk, uvt, gq, gkv, tret, tmla, qk_scale)
    rk_ctx, rv_ctx, k_ctx, vt_ctx = _inproj_ctx(ctx, m_ctx[0], m_ctx[1], g1, w_ctx, uk, uvt, gkv)

    lg = jnp.stack([jax.nn.log_sigmoid(ret_decay_fwd[0].astype(F32)),
                    jax.nn.log_sigmoid(ret_decay_bwd[0].astype(F32))])
    ret_o = _retention(lg, rq, rk, rv, rk_ctx, rv_ctx)
    att_o = _attention(q, k_lat, k_ctx, vt_lat, vt_ctx)

    wrt = jnp.pad(router_w[0], ((0, 0), (0, LANES - N_EXPERTS))).astype(BF16)
    brt = jnp.pad(router_b[0], (0, LANES - N_EXPERTS)).reshape(1, LANES)
    x1, h2, logits = _merge(x, ret_o, sgate, att_o, sgr, sgm, m_lat[2], m_lat[3], m_lat[4],
                            norm2_g[0].reshape(1, d), w_branch_ret[0].astype(BF16),
                            w_branch_mla[0].astype(BF16), w_out[0].astype(BF16), wrt, brt)

    n_tok = b * l
    idx, gates, rank, counts = _router(logits.reshape(n_tok, LANES))
    dest, block_e, n_used, fill, n_rows = _layout(idx, rank, counts, n_tok)
    xs = _dispatch(fill, dest, h2.reshape(_tile_rows_shape(n_tok, d)), n_rows)
    ys = _experts(block_e, n_used, xs, exp_w_gu[0], exp_b_gu[0], exp_w_down[0], exp_b_down[0])
    return _combine(dest, x1, gates[:TOP_K].T, m_lat[5], final_norm_g.reshape(1, d), ys)
```

```python
import functools

import numpy as np
import jax
import jax.numpy as jnp
from jax import lax
from jax.experimental import pallas as pl
from jax.experimental.pallas import tpu as pltpu

GRID_W = 64
N_RET_HEADS = 4
RET_QK_DIM = 256
RET_V_DIM = 512
N_MLA_HEADS = 8
MLA_Q_RANK = 384
MLA_KV_RANK = 256
MLA_QK_NOPE = 128
MLA_QK_ROPE = 64
MLA_V_DIM = 128
N_EXPERTS = 32
TOP_K = 4
SWIGLU_LIMIT = 7.0
SWIGLU_ALPHA = 1.702
ROPE_BASE = 10000.0
EPS = 1e-6

LANES = 128
MLA_HEAD_PAD = 2 * LANES
RET_CHUNK = 256
VMEM_LIMIT = 56 * 1024 * 1024
INPROJ_TILE = 256
MERGE_TILE = 512
MERGE_PARTS = 2

F32 = jnp.float32
BF16 = jnp.bfloat16


def _cparams(sem):
    return pltpu.CompilerParams(dimension_semantics=sem, vmem_limit_bytes=VMEM_LIMIT)


def _resident(shape):
    nd = len(shape)
    return pl.BlockSpec(shape, lambda *_: (0,) * nd, pipeline_mode=pl.Buffered(1))


def _dot(a, b):
    return jnp.dot(a, b, preferred_element_type=F32)


def _dot_nt(a, b):
    return lax.dot_general(a, b, (((1,), (1,)), ((), ())), preferred_element_type=F32)


def _dot_tn(a, b):
    return lax.dot_general(a, b, (((0,), (0,)), ((), ())), preferred_element_type=F32)


SUBLANES = 8


def _tile_rows_shape(n, d):
    assert d == SUBLANES * LANES
    return (n * SUBLANES, LANES)


def _store_tile_rows(ref, lead, val):
    rows = val.shape[0]
    for s in range(SUBLANES):
        ref[lead + (pl.ds(s, rows, stride=SUBLANES), slice(None))] = val[:, s * LANES:(s + 1) * LANES]


def _load_tile_rows(ref, lead, rows):
    return jnp.concatenate([ref[lead + (pl.ds(s, rows, stride=SUBLANES), slice(None))]
                            for s in range(SUBLANES)], axis=1)


def _tile_row(ref, r):
    return ref.at[pl.ds(pl.multiple_of(r * SUBLANES, SUBLANES), SUBLANES), :]


def _rms(x):
    return x * lax.rsqrt(jnp.mean(x * x, axis=-1, keepdims=True) + EPS)


def _sigmoid(x):
    return 1.0 / (1.0 + jnp.exp(-x))


def _ada_kernel(c_ref, w_ref, b_ref, o_ref):
    c = c_ref[...]
    s = c * _sigmoid(c)
    o_ref[...] = jnp.dot(s, w_ref[...], preferred_element_type=F32,
                         precision=lax.Precision.HIGHEST) + b_ref[...]


def _ada_mod(cvec, w, b):
    rows, d = cvec.shape
    n = w.shape[1]
    tn = 1024
    return pl.pallas_call(
        _ada_kernel,
        out_shape=jax.ShapeDtypeStruct((rows, n), F32),
        grid=(n // tn,),
        in_specs=[pl.BlockSpec((rows, d), lambda j: (0, 0)),
                  pl.BlockSpec((d, tn), lambda j: (0, j)),
                  pl.BlockSpec((1, tn), lambda j: (0, j))],
        out_specs=pl.BlockSpec((rows, tn), lambda j: (0, j)),
        compiler_params=_cparams(("arbitrary",)),
        name="ada_mod",
    )(cvec, w, b.reshape(1, n))


def _rope_tables(seq_len, qk_scale):
    pos = np.arange(seq_len)
    rows = (pos // GRID_W).astype(np.float32)
    cols = (pos % GRID_W).astype(np.float32)

    def angles(p, half):
        freqs = (np.float32(ROPE_BASE) ** (-np.arange(half, dtype=np.float32) / np.float32(half))).astype(np.float32)
        return (p[:, None] * freqs[None, :]).astype(np.float32)

    a_r, a_c = angles(rows, 64), angles(cols, 64)
    ret = np.concatenate([np.cos(a_r), np.cos(a_r), np.cos(a_c), np.cos(a_c),
                          -np.sin(a_r), np.sin(a_r), -np.sin(a_c), np.sin(a_c)], axis=1)
    b_r, b_c = angles(rows, 16), angles(cols, 16)
    z32 = np.zeros((seq_len, 32), np.float32)
    z64 = np.zeros((seq_len, 64), np.float32)
    cos = np.concatenate([np.cos(b_r), np.cos(b_c), np.cos(b_r), np.cos(b_c), z64], axis=1)
    sin_up = np.concatenate([-np.sin(b_r), -np.sin(b_c), z32, z64], axis=1)
    sin_dn = np.concatenate([z32, np.sin(b_r), np.sin(b_c), z64], axis=1)
    mla_k = np.concatenate([cos, sin_up, sin_dn], axis=1)
    mla = np.concatenate([mla_k, mla_k * np.float32(qk_scale)], axis=1)
    return jnp.asarray(ret, F32), jnp.asarray(mla, F32)


def _rope_ret(t, tab, parity):
    cos = tab[:, parity * LANES:(parity + 1) * LANES]
    sin = tab[:, (2 + parity) * LANES:(3 + parity) * LANES]
    return t * cos + pltpu.roll(t, 64, 1) * sin


def _rope_mla(t, tab, base):
    cos = tab[:, base:base + LANES]
    sin_up = tab[:, base + LANES:base + 2 * LANES]
    sin_dn = tab[:, base + 2 * LANES:base + 3 * LANES]
    return t * cos + pltpu.roll(t, 96, 1) * sin_up + pltpu.roll(t, 32, 1) * sin_dn


_C_Q, _C_K, _C_V, _C_G = 0, 1024, 2048, 4096
_C_CQ, _C_CKV, _C_KR, _C_GR, _C_GM, _C_END = 6144, 6528, 6784, 6912, 7936, 8960
_X_K, _X_V, _X_CKV, _X_KR, _X_END = 0, 1024, 3072, 3328, 3456


def _norm_mod(x, g, shift, scale):
    return (_rms(x) * g) * (1.0 + scale) + shift


def _mla_kv(ckv_acc, kr, gkv_ref, wuk_ref, wuvt_ref, k_ref, vt_ref):
    ckvn = (_rms(ckv_acc) * gkv_ref[...]).astype(BF16)
    kn = _dot(ckvn, wuk_ref[...])
    krb = kr.astype(BF16)
    for hh in range(N_MLA_HEADS):
        k_ref[0, :, hh * MLA_HEAD_PAD:hh * MLA_HEAD_PAD + LANES] = kn[:, hh * LANES:(hh + 1) * LANES].astype(BF16)
        k_ref[0, :, hh * MLA_HEAD_PAD + LANES:(hh + 1) * MLA_HEAD_PAD] = krb
    vt_ref[0] = _dot_nt(wuvt_ref[...], ckvn).astype(BF16)


def _inproj_lat_kernel(qk_scale, x_ref, shift_ref, scale_ref, g_ref, w_ref, wuq_ref, wuk_ref, wuvt_ref, gq_ref,
                       gkv_ref, tret_ref, tmla_ref,
                       rq_ref, rk_ref, rv_ref, sgate_ref, q_ref, k_ref, vt_ref, sgr_ref, sgm_ref):
    h = _norm_mod(x_ref[0], g_ref[...], shift_ref[0], scale_ref[0]).astype(BF16)
    tret = tret_ref[...]
    tmla = tmla_ref[...]

    def proj(c0, n):
        return _dot(h, w_ref[:, c0:c0 + n])

    cq_acc = proj(_C_CQ, MLA_Q_RANK)
    ckv_acc = proj(_C_CKV, MLA_KV_RANK)
    kr_acc = proj(_C_KR, LANES)

    for base, out in ((_C_Q, rq_ref), (_C_K, rk_ref)):
        for j in range(2):
            acc = proj(base + j * 512, 512)
            for blk in range(4):
                t = acc[:, blk * LANES:(blk + 1) * LANES]
                col = j * 512 + blk * LANES
                out[0, :, col:col + LANES] = _rope_ret(t, tret, blk % 2).astype(BF16)

    cqn = (_rms(cq_acc) * gq_ref[...]).astype(BF16)
    for j in range(4):
        acc = _dot(cqn, wuq_ref[:, j * 512:(j + 1) * 512])
        for blk in range(4):
            t = acc[:, blk * LANES:(blk + 1) * LANES]
            col = j * 512 + blk * LANES
            if blk % 2 == 0:
                q_ref[0, :, col:col + LANES] = (t * qk_scale).astype(BF16)
            else:
                q_ref[0, :, col:col + LANES] = _rope_mla(t, tmla, 3 * LANES).astype(BF16)
    _mla_kv(ckv_acc, _rope_mla(kr_acc, tmla, 0), gkv_ref, wuk_ref, wuvt_ref, k_ref, vt_ref)

    for j in range(4):
        a = proj(_C_G + j * 512, 512)
        sgate_ref[0, :, j * 512:(j + 1) * 512] = (a * _sigmoid(a)).astype(BF16)
    for j in range(2):
        a = proj(_C_GR + j * 512, 512)
        sgr_ref[0, :, j * 512:(j + 1) * 512] = _sigmoid(a).astype(BF16)
    for j in range(2):
        a = proj(_C_GM + j * 512, 512)
        sgm_ref[0, :, j * 512:(j + 1) * 512] = _sigmoid(a).astype(BF16)
    for j in range(4):
        rv_ref[0, :, j * 512:(j + 1) * 512] = proj(_C_V + j * 512, 512).astype(BF16)


def _inproj_ctx_kernel(x_ref, shift_ref, scale_ref, g_ref, w_ref, wuk_ref, wuvt_ref, gkv_ref,
                       rk_ref, rv_ref, k_ref, vt_ref):
    h = _norm_mod(x_ref[0], g_ref[...], shift_ref[...], scale_ref[...]).astype(BF16)

    def proj(c0, n):
        return _dot(h, w_ref[:, c0:c0 + n])

    for j in range(2):
        rk_ref[0, :, j * 512:(j + 1) * 512] = proj(_X_K + j * 512, 512).astype(BF16)
    for j in range(4):
        rv_ref[0, :, j * 512:(j + 1) * 512] = proj(_X_V + j * 512, 512).astype(BF16)
    _mla_kv(proj(_X_CKV, MLA_KV_RANK), proj(_X_KR, LANES), gkv_ref, wuk_ref, wuvt_ref, k_ref, vt_ref)


def _prep_weights(w_in, w_uq, w_ukv):
    nq = N_RET_HEADS * RET_QK_DIM
    nv = N_RET_HEADS * RET_V_DIM
    d = w_in.shape[0]
    sizes = (nq, nq, nv, nv, MLA_Q_RANK, MLA_KV_RANK, MLA_QK_ROPE, d, d)
    offs = np.cumsum((0,) + sizes)
    wq, wk, wv, wg, wcq, wckv, wkr, wgr, wgm = [w_in[:, offs[i]:offs[i + 1]] for i in range(9)]
    wk = wk * (RET_QK_DIM ** -0.5)
    perm = np.concatenate([np.arange(0, 16), np.arange(32, 48), np.arange(16, 32), np.arange(48, 64)])
    wkr = jnp.pad(wkr[:, perm], ((0, 0), (0, LANES - MLA_QK_ROPE)))
    w_lat = jnp.concatenate([wq, wk, wv, wg, wcq, wckv, wkr, wgr, wgm], axis=1).astype(BF16)
    w_ctx = jnp.concatenate([wk, wv, wckv, wkr], axis=1).astype(BF16)
    uq = w_uq.reshape(MLA_Q_RANK, N_MLA_HEADS, MLA_QK_NOPE + MLA_QK_ROPE)
    uq = jnp.concatenate([uq[:, :, :MLA_QK_NOPE], uq[:, :, MLA_QK_NOPE:][:, :, perm],
                          jnp.zeros((MLA_Q_RANK, N_MLA_HEADS, LANES - MLA_QK_ROPE), w_uq.dtype)], axis=2)
    uq = uq.reshape(MLA_Q_RANK, N_MLA_HEADS * MLA_HEAD_PAD).astype(BF16)
    ukv = w_ukv.reshape(MLA_KV_RANK, N_MLA_HEADS, MLA_QK_NOPE + MLA_V_DIM)
    uk = ukv[:, :, :MLA_QK_NOPE].reshape(MLA_KV_RANK, -1).astype(BF16)
    uvt = ukv[:, :, MLA_QK_NOPE:].reshape(MLA_KV_RANK, -1).T.astype(BF16)
    return w_lat, w_ctx, uq, uk, uvt


def _tok_spec(tm, n):
    return pl.BlockSpec((1, tm, n), lambda i, j: (i, j, 0))


def _vt_spec(tm):
    return pl.BlockSpec((1, N_MLA_HEADS * MLA_V_DIM, tm), lambda i, j: (i, 0, j))


def _inproj_lat(x, shift, scale, g, w_lat, uq, uk, uvt, gq, gkv, tret, tmla, qk_scale):
    b, l, d = x.shape
    tm = INPROJ_TILE
    vec = pl.BlockSpec((1, 1, d), lambda i, j: (i, 0, 0))
    out_w = (1024, 1024, 2048, 2048, 2048, 2048, None, 1024, 1024)
    nvt = N_MLA_HEADS * MLA_V_DIM
    return pl.pallas_call(
        functools.partial(_inproj_lat_kernel, qk_scale),
        out_shape=[jax.ShapeDtypeStruct((b, nvt, l) if n is None else (b, l, n), BF16) for n in out_w],
        grid=(b, l // tm),
        in_specs=[_tok_spec(tm, d), vec, vec, _resident((1, d)), _resident(w_lat.shape), _resident(uq.shape),
                  _resident(uk.shape), _resident(uvt.shape), _resident((1, MLA_Q_RANK)),
                  _resident((1, MLA_KV_RANK)),
                  pl.BlockSpec((tm, tret.shape[1]), lambda i, j: (j, 0)),
                  pl.BlockSpec((tm, tmla.shape[1]), lambda i, j: (j, 0))],
        out_specs=[_vt_spec(tm) if n is None else _tok_spec(tm, n) for n in out_w],
        compiler_params=_cparams(("arbitrary", "arbitrary")),
        name="inproj_lat",
    )(x, shift, scale, g, w_lat, uq, uk, uvt, gq, gkv, tret, tmla)


def _inproj_ctx(ctx, shift, scale, g, w_ctx, uk, uvt, gkv):
    b, l, d = ctx.shape
    tm = 256
    out_w = (1024, 2048, 2048, None)
    nvt = N_MLA_HEADS * MLA_V_DIM
    return pl.pallas_call(
        _inproj_ctx_kernel,
        out_shape=[jax.ShapeDtypeStruct((b, nvt, l) if n is None else (b, l, n), BF16) for n in out_w],
        grid=(b, l // tm),
        in_specs=[_tok_spec(tm, d), _resident((1, d)), _resident((1, d)), _resident((1, d)),
                  _resident(w_ctx.shape), _resident(uk.shape), _resident(uvt.shape),
                  _resident((1, MLA_KV_RANK))],
        out_specs=[_vt_spec(tm) if n is None else _tok_spec(tm, n) for n in out_w],
        compiler_params=_cparams(("arbitrary", "arbitrary")),
        name="inproj_ctx",
    )(ctx, shift, scale, g, w_ctx, uk, uvt, gkv)


def _retention_kernel(n_chunks, n_ctx_chunks, lg_ref, q_ref, k_ref, v_ref, kc_ref, vc_ref, o_ref,
                      ob_ref, sf_ref, sb_ref, dmat_ref, dec_ref):
    c = RET_CHUNK
    head = pl.program_id(1)
    lgf = lg_ref[0, head]
    lgb = lg_ref[1, head]
    ri = lax.broadcasted_iota(jnp.int32, (c, c), 0).astype(F32)
    ci = lax.broadcasted_iota(jnp.int32, (c, c), 1).astype(F32)
    diff = ri - ci
    dmat_ref[...] = jnp.where(diff >= 0, jnp.exp(lgf * jnp.maximum(diff, 0.0)),
                              jnp.exp(lgb * jnp.maximum(-diff, 0.0)))
    rk = lax.broadcasted_iota(jnp.int32, (c, RET_QK_DIM), 0).astype(F32)
    dec_ref[0] = jnp.exp(lgf * (rk + 1.0))
    dec_ref[1] = jnp.exp(lgf * (c - 1.0 - rk))
    dec_ref[2] = jnp.exp(lgb * (c - rk))
    dec_ref[3] = jnp.exp(lgb * rk)
    cdf = jnp.exp(jnp.full((1, RET_V_DIM), lgf * c, F32))
    cdb = jnp.exp(jnp.full((1, RET_V_DIM), lgb * c, F32))

    def scaled(t, which):
        return (t.astype(F32) * dec_ref[which]).astype(BF16)

    sf_ref[...] = jnp.zeros_like(sf_ref)
    sb_ref[...] = jnp.zeros_like(sb_ref)
    for n in range(n_ctx_chunks):
        sl = slice(n * c, (n + 1) * c)
        sf_ref[...] = sf_ref[...] * cdf + _dot_tn(scaled(kc_ref[0, sl, :], 1), vc_ref[0, sl, :])
    for n in reversed(range(n_ctx_chunks)):
        sl = slice(n * c, (n + 1) * c)
        sb_ref[...] = sb_ref[...] * cdb + _dot_tn(scaled(kc_ref[0, sl, :], 3), vc_ref[0, sl, :])

    def step(i, finish):
        rb = pl.multiple_of((n_chunks - 1 - i) * c, c)
        rf = pl.multiple_of(i * c, c)
        qf = q_ref[0, pl.ds(rf, c), :]
        kf = k_ref[0, pl.ds(rf, c), :]
        vf = v_ref[0, pl.ds(rf, c), :]
        qb = q_ref[0, pl.ds(rb, c), :]
        kb = k_ref[0, pl.ds(rb, c), :]
        vb = v_ref[0, pl.ds(rb, c), :]
        s = _dot_nt(qf, kf)
        o_b = _dot(scaled(qb, 2), sb_ref[...].astype(BF16))
        sb_new = _dot_tn(scaled(kb, 3), vb)
        o_f = _dot(scaled(qf, 0), sf_ref[...].astype(BF16))
        sf_new = _dot_tn(scaled(kf, 1), vf)
        o_f = o_f + _dot((s * dmat_ref[...]).astype(BF16), vf)
        sb_ref[...] = sb_ref[...] * cdb + sb_new
        sf_ref[...] = sf_ref[...] * cdf + sf_new
        if finish:
            o_ref[0, pl.ds(rb, c), :] = _rms(o_b + ob_ref[pl.ds(rb, c), :]).astype(BF16)
            o_ref[0, pl.ds(rf, c), :] = _rms(o_f + ob_ref[pl.ds(rf, c), :]).astype(BF16)
        else:
            ob_ref[pl.ds(rb, c), :] = o_b
            ob_ref[pl.ds(rf, c), :] = o_f

    half = n_chunks // 2
    lax.fori_loop(0, half, lambda i, carry: (step(i, False), carry)[1], 0)
    lax.fori_loop(half, n_chunks, lambda i, carry: (step(i, True), carry)[1], 0)


def _retention(lg, rq, rk, rv, rk_ctx, rv_ctx):
    b, l, _ = rq.shape
    lc = rk_ctx.shape[1]
    c = RET_CHUNK
    assert l % (2 * c) == 0 and lc % c == 0
    qk = lambda n: pl.BlockSpec((1, n, RET_QK_DIM), lambda i, h: (i, 0, h))
    vv = lambda n: pl.BlockSpec((1, n, RET_V_DIM), lambda i, h: (i, 0, h))
    return pl.pallas_call(
        functools.partial(_retention_kernel, l // c, lc // c),
        out_shape=jax.ShapeDtypeStruct((b, l, N_RET_HEADS * RET_V_DIM), BF16),
        grid=(b, N_RET_HEADS),
        in_specs=[pl.BlockSpec(memory_space=pltpu.SMEM), qk(l), qk(l), vv(l), qk(lc), vv(lc)],
        out_specs=vv(l),
        scratch_shapes=[pltpu.VMEM((l, RET_V_DIM), F32),
                        pltpu.VMEM((RET_QK_DIM, RET_V_DIM), F32),
                        pltpu.VMEM((RET_QK_DIM, RET_V_DIM), F32),
                        pltpu.VMEM((c, c), F32),
                        pltpu.VMEM((4, c, RET_QK_DIM), F32)],
        compiler_params=_cparams(("arbitrary", "arbitrary")),
        name="retention",
    )(lg, rq, rk, rv, rk_ctx, rv_ctx)


ATTN_CHAIN = 256
ATTN_CHAINS = 8


ATTN_KEY_BLOCK = 1024


def _attention_kernel(q_ref, kl_ref, kc_ref, vtl_ref, vtc_ref, o_ref):
    l = kl_ref.shape[1]
    lc = kc_ref.shape[1]
    blocks = [(kc_ref, vtc_ref, 0, lc)]
    blocks += [(kl_ref, vtl_ref, r0, ATTN_KEY_BLOCK) for r0 in range(0, l, ATTN_KEY_BLOCK)]
    chains = range(ATTN_CHAINS)
    rows = [slice(ch * ATTN_CHAIN, (ch + 1) * ATTN_CHAIN) for ch in chains]
    q = [q_ref[0, rows[ch], :] for ch in chains]

    def scores(ch, j):
        k_ref, _, r0, n = blocks[j]
        return _dot_nt(k_ref[0, r0:r0 + n, :], q[ch])

    s = [scores(ch, 0) for ch in chains]
    m = [None] * ATTN_CHAINS
    denom = [None] * ATTN_CHAINS
    acc = [None] * ATTN_CHAINS
    for j in range(len(blocks)):
        _, vt_ref, r0, n = blocks[j]
        for ch in chains:
            s_cur = s[ch]
            if j + 1 < len(blocks):
                s[ch] = scores(ch, j + 1)
            m_blk = jnp.max(s_cur, axis=0, keepdims=True)
            if j == 0:
                m[ch] = m_blk
                p = jnp.exp2(s_cur - m_blk)
                denom[ch] = jnp.sum(p, axis=0, keepdims=True)
                acc[ch] = _dot(vt_ref[0, :, r0:r0 + n], p.astype(BF16))
            else:
                m_new = jnp.maximum(m[ch], m_blk)
                alpha = jnp.exp2(m[ch] - m_new)
                p = jnp.exp2(s_cur - m_new)
                denom[ch] = denom[ch] * alpha + jnp.sum(p, axis=0, keepdims=True)
                acc[ch] = acc[ch] * alpha + _dot(vt_ref[0, :, r0:r0 + n], p.astype(BF16))
                m[ch] = m_new
    for ch in chains:
        o_ref[0, rows[ch], :] = (acc[ch] / denom[ch]).T.astype(BF16)


def _attention(q, k_lat, k_ctx, vt_lat, vt_ctx):
    b, l, _ = q.shape
    lc = k_ctx.shape[1]
    tq = ATTN_CHAIN * ATTN_CHAINS
    return pl.pallas_call(
        _attention_kernel,
        out_shape=jax.ShapeDtypeStruct((b, l, N_MLA_HEADS * MLA_V_DIM), BF16),
        grid=(b, N_MLA_HEADS, l // tq),
        in_specs=[pl.BlockSpec((1, tq, MLA_HEAD_PAD), lambda i, h, j: (i, j, h)),
                  pl.BlockSpec((1, l, MLA_HEAD_PAD), lambda i, h, j: (i, 0, h)),
                  pl.BlockSpec((1, lc, MLA_HEAD_PAD), lambda i, h, j: (i, 0, h)),
                  pl.BlockSpec((1, MLA_V_DIM, l), lambda i, h, j: (i, h, 0)),
                  pl.BlockSpec((1, MLA_V_DIM, lc), lambda i, h, j: (i, h, 0))],
        out_specs=pl.BlockSpec((1, tq, MLA_V_DIM), lambda i, h, j: (i, j, h)),
        compiler_params=_cparams(("arbitrary", "arbitrary", "arbitrary")),
        name="attention",
    )(q, k_lat, k_ctx, vt_lat, vt_ctx)


def _merge_kernel(x_ref, ret_ref, sgate_ref, att_ref, sgr_ref, sgm_ref, gate1_ref, shift2_ref, scale2_ref,
                  g2_ref, wr_ref, wm_ref, wo_ref, wrt_ref, brt_ref, x1_ref, h2_ref, logit_ref):
    half = x_ref.shape[1] // MERGE_PARTS
    parts = [slice(p * half, (p + 1) * half) for p in range(MERGE_PARTS)]
    branch = []
    for rows in parts:
        r = (ret_ref[0, rows, :].astype(F32) * sgate_ref[0, rows, :].astype(F32)).astype(BF16)
        branch.append((_dot(r, wr_ref[...]), _dot(att_ref[0, rows, :], wm_ref[...])))
    ys = []
    for rows, (a_ret, a_mla) in zip(parts, branch):
        merged = sgr_ref[0, rows, :].astype(F32) * a_ret + sgm_ref[0, rows, :].astype(F32) * a_mla
        ys.append(_dot(merged.astype(BF16), wo_ref[...]))
    for rows, y in zip(parts, ys):
        x1 = x_ref[0, rows, :] + gate1_ref[0] * y
        x1_ref[0, rows, :] = x1
        h2 = _norm_mod(x1, g2_ref[...], shift2_ref[0], scale2_ref[0])
        for s in range(SUBLANES):
            h2_ref[0, pl.ds(rows.start * SUBLANES + s, half, stride=SUBLANES), :] = h2[:, s * LANES:(s + 1) * LANES]
        logit_ref[0, rows, :] = _dot(h2.astype(BF16), wrt_ref[...]) + brt_ref[...]


def _merge(x, ret_o, sgate, att_o, sgr, sgm, gate1, shift2, scale2, g2, wr, wm, wo, wrt, brt):
    b, l, d = x.shape
    tm = MERGE_TILE
    tok = lambda n: pl.BlockSpec((1, tm, n), lambda i, j: (i, j, 0))
    vec = pl.BlockSpec((1, 1, d), lambda i, j: (i, 0, 0))
    return pl.pallas_call(
        _merge_kernel,
        out_shape=[jax.ShapeDtypeStruct((b, l, d), F32),
                   jax.ShapeDtypeStruct((b,) + _tile_rows_shape(l, d), F32),
                   jax.ShapeDtypeStruct((b, l, LANES), F32)],
        grid=(b, l // tm),
        in_specs=[tok(d), tok(ret_o.shape[2]), tok(sgate.shape[2]), tok(att_o.shape[2]), tok(d), tok(d),
                  vec, vec, vec, _resident((1, d)), _resident(wr.shape), _resident(wm.shape),
                  _resident(wo.shape), _resident(wrt.shape), _resident(brt.shape)],
        out_specs=[tok(d), pl.BlockSpec((1,) + _tile_rows_shape(tm, d), lambda i, j: (i, j, 0)), tok(LANES)],
        compiler_params=_cparams(("arbitrary", "arbitrary")),
        name="merge",
    )(x, ret_o, sgate, att_o, sgr, sgm, gate1, shift2, scale2, g2, wr, wm, wo, wrt, brt)


EXPERT_BLOCK = 512
EXPERT_PARTS = 1


def _expert_kernel(be_ref, nb_ref, x_ref, wgu_ref, bgu_ref, wd_ref, bd_ref, y_ref, wgu_bf, wd_bf):
    step = pl.program_id(0)

    @pl.when(step >= nb_ref[0])
    def _():
        y_ref[...] = jnp.zeros_like(y_ref)

    @pl.when((step < nb_ref[0]) & ((step == 0) | (be_ref[step] != be_ref[jnp.maximum(step - 1, 0)])))
    def _():
        cw = 4 * LANES
        for c0 in range(0, wgu_bf.shape[1], cw):
            wgu_bf[:, c0:c0 + cw] = wgu_ref[0, :, c0:c0 + cw].astype(BF16)
        for c0 in range(0, wd_bf.shape[1], cw):
            wd_bf[:, c0:c0 + cw] = wd_ref[0, :, c0:c0 + cw].astype(BF16)

    @pl.when(step < nb_ref[0])
    def _():
        f = wd_bf.shape[0]
        rows = EXPERT_BLOCK // EXPERT_PARTS
        gu = []
        for p in range(EXPERT_PARTS):
            x = jnp.concatenate([x_ref[pl.ds(p * rows * SUBLANES + s, rows, stride=SUBLANES), :]
                                 for s in range(SUBLANES)], axis=1).astype(BF16)
            gu.append((_dot(x, wgu_bf[:, :f]) + bgu_ref[0, :, :f], _dot(x, wgu_bf[:, f:]) + bgu_ref[0, :, f:]))
        for p, (gate, up) in enumerate(gu):
            gate = jnp.minimum(gate, SWIGLU_LIMIT)
            up = jnp.clip(up, -SWIGLU_LIMIT, SWIGLU_LIMIT)
            glu = gate * _sigmoid(SWIGLU_ALPHA * gate)
            act = ((up + 1.0) * glu).astype(BF16)
            y = _dot(act, wd_bf[...]) + bd_ref[0]
            for s in range(SUBLANES):
                y_ref[pl.ds(p * rows * SUBLANES + s, rows, stride=SUBLANES), :] = y[:, s * LANES:(s + 1) * LANES]


def _experts(block_e, n_used, xs, w_gu, b_gu, w_down, b_down):
    n_rows = xs.shape[0] // SUBLANES
    e, d, f2 = w_gu.shape
    f = f2 // 2
    bm = EXPERT_BLOCK
    blk = _tile_rows_shape(bm, d)
    return pl.pallas_call(
        _expert_kernel,
        out_shape=jax.ShapeDtypeStruct(xs.shape, F32),
        grid_spec=pltpu.PrefetchScalarGridSpec(
            num_scalar_prefetch=2,
            grid=(n_rows // bm,),
            in_specs=[pl.BlockSpec(blk, lambda i, be, nb: (jnp.minimum(i, nb[0] - 1), 0)),
                      pl.BlockSpec((1, d, f2), lambda i, be, nb: (be[i], 0, 0)),
                      pl.BlockSpec((1, 1, f2), lambda i, be, nb: (be[i], 0, 0)),
                      pl.BlockSpec((1, f, d), lambda i, be, nb: (be[i], 0, 0)),
                      pl.BlockSpec((1, 1, d), lambda i, be, nb: (be[i], 0, 0))],
            out_specs=pl.BlockSpec(blk, lambda i, be, nb: (i, 0)),
            scratch_shapes=[pltpu.VMEM((d, f2), BF16), pltpu.VMEM((f, d), BF16)]),
        compiler_params=_cparams(("arbitrary",)),
        name="experts",
    )(block_e, n_used, xs, w_gu, b_gu.reshape(e, 1, f2), w_down, b_down.reshape(e, 1, d))


ROUTE_TILE = 512
ROUTE_ROWS = 8
DMA_UNROLL = 8


def _router_kernel(logit_ref, idx_ref, gate_ref, rank_ref, cnt_ref, base_ref, tri_ref):
    tm = logit_ref.shape[0]
    step = pl.program_id(0)

    @pl.when(step == 0)
    def _():
        base_ref[...] = jnp.zeros_like(base_ref)
        r = lax.broadcasted_iota(jnp.int32, (tm, tm), 0)
        c = lax.broadcasted_iota(jnp.int32, (tm, tm), 1)
        tri_ref[...] = jnp.where(r < c, 1.0, 0.0).astype(BF16)

    v = logit_ref[...].T[:N_EXPERTS, :]
    eid = lax.broadcasted_iota(jnp.int32, (N_EXPERTS, tm), 0)
    onehot = jnp.zeros((N_EXPERTS, tm), F32)
    vals, sels = [], []
    for kk in range(TOP_K):
        mx = jnp.max(v, axis=0, keepdims=True)
        ik = jnp.min(jnp.where(v == mx, eid, N_EXPERTS), axis=0, keepdims=True)
        sel = eid == ik
        idx_ref[kk:kk + 1, :] = ik
        vals.append(mx)
        sels.append(sel)
        onehot = onehot + jnp.where(sel, 1.0, 0.0)
        v = jnp.where(sel, -jnp.inf, v)
    ex = [jnp.exp(val - vals[0]) for val in vals]
    tot = ex[0] + ex[1] + ex[2] + ex[3]
    for kk in range(TOP_K):
        gate_ref[kk:kk + 1, :] = ex[kk] / tot
    rank_e = base_ref[:, 0:1] + _dot(onehot.astype(BF16), tri_ref[...])
    for kk in range(TOP_K):
        rank_ref[kk:kk + 1, :] = jnp.sum(jnp.where(sels[kk], rank_e, 0.0), axis=0, keepdims=True).astype(jnp.int32)
    zero_i = jnp.zeros((ROUTE_ROWS - TOP_K, tm), jnp.int32)
    idx_ref[TOP_K:, :] = zero_i
    rank_ref[TOP_K:, :] = zero_i
    gate_ref[TOP_K:, :] = jnp.zeros((ROUTE_ROWS - TOP_K, tm), F32)
    base_ref[...] = base_ref[...] + jnp.sum(onehot, axis=1, keepdims=True)
    cnt_ref[...] = base_ref[...]


def _router(logits):
    n_tok = logits.shape[0]
    tm = ROUTE_TILE
    tab = lambda: pl.BlockSpec((ROUTE_ROWS, tm), lambda i: (0, i))
    return pl.pallas_call(
        _router_kernel,
        out_shape=[jax.ShapeDtypeStruct((ROUTE_ROWS, n_tok), jnp.int32),
                   jax.ShapeDtypeStruct((ROUTE_ROWS, n_tok), F32),
                   jax.ShapeDtypeStruct((ROUTE_ROWS, n_tok), jnp.int32),
                   jax.ShapeDtypeStruct((N_EXPERTS, LANES), F32)],
        grid=(n_tok // tm,),
        in_specs=[pl.BlockSpec((tm, LANES), lambda i: (i, 0))],
        out_specs=[tab(), tab(), tab(), pl.BlockSpec((N_EXPERTS, LANES), lambda i: (0, 0))],
        scratch_shapes=[pltpu.VMEM((N_EXPERTS, LANES), F32), pltpu.VMEM((tm, tm), BF16)],
        compiler_params=_cparams(("arbitrary",)),
        name="router",
    )(logits)


def _layout(idx, rank, counts, n_tok):
    bm = EXPERT_BLOCK
    counts = counts[:, 0].astype(jnp.int32)
    padded = (counts + bm - 1) // bm * bm
    pad_end = jnp.cumsum(padded)
    pad_start = pad_end - padded
    n_rows = n_tok * TOP_K + N_EXPERTS * bm
    n_blocks = n_rows // bm
    blk_start = jnp.arange(n_blocks, dtype=jnp.int32) * bm
    block_e = jnp.minimum(jnp.sum(blk_start[:, None] >= pad_end[None, :], axis=1), N_EXPERTS - 1).astype(jnp.int32)
    n_used = (pad_end[-1] // bm).astype(jnp.int32).reshape(1)
    start_of = jnp.zeros_like(idx)
    for e in range(N_EXPERTS):
        start_of = jnp.where(idx == e, pad_start[e], start_of)
    dest = start_of + rank
    fill = jnp.stack([pad_start + counts, pad_end]).astype(jnp.int32)
    return dest, block_e, n_used, fill, n_rows


def _dispatch_kernel(fill_ref, dest_ref, h_ref, xs_ref, hbuf_ref, zero_ref, lsem, sem, zsem):
    tm = dest_ref.shape[2] // TOP_K
    step = pl.program_id(0)
    last = pl.num_programs(0) - 1
    par = lax.rem(step, 2)
    slot = lax.rem(step, 3)

    def tile_copy(i, sl):
        rows = tm * SUBLANES
        return pltpu.make_async_copy(h_ref.at[pl.ds(pl.multiple_of(i * rows, rows), rows), :],
                                     hbuf_ref.at[sl], lsem.at[sl])

    @pl.when(step == 0)
    def _():
        tile_copy(0, 0).start()

    @pl.when(step < last)
    def _():
        tile_copy(step + 1, lax.rem(step + 1, 3)).start()

    tile_copy(step, slot).wait()
    src_ref = hbuf_ref.at[slot]

    def row_copy(t, kk, sl, dst=None):
        dst = dest_ref[0, 0, t * TOP_K + kk] if dst is None else dst
        return pltpu.make_async_copy(_tile_row(src_ref, t), _tile_row(xs_ref, dst), sem.at[sl])

    def issue(g, carry):
        t0 = g * DMA_UNROLL
        dst = [[dest_ref[0, 0, (t0 + u) * TOP_K + kk] for kk in range(TOP_K)] for u in range(DMA_UNROLL)]
        for u in range(DMA_UNROLL):
            for kk in range(TOP_K):
                row_copy(t0 + u, kk, par, dst[u][kk]).start(priority=kk % 2)
        return carry

    lax.fori_loop(0, tm // DMA_UNROLL, issue, 0)

    def drain(sl):
        def body(t, carry):
            for kk in range(TOP_K):
                row_copy(t, kk, sl).wait()
            return carry

        lax.fori_loop(0, tm, body, 0, unroll=DMA_UNROLL)

    @pl.when(step > 0)
    def _():
        drain(1 - par)

    @pl.when(step == last)
    def _():
        drain(par)

    @pl.when(step == last)
    def _():
        zero_ref[...] = jnp.zeros_like(zero_ref)

        def zero_copy(r):
            return pltpu.make_async_copy(zero_ref, _tile_row(xs_ref, r), zsem)

        def per_expert(e, carry):
            lo = fill_ref[0, e]
            hi = fill_ref[1, e]
            lax.fori_loop(lo, hi, lambda r, c: (zero_copy(r).start(), c)[1], 0)
            lax.fori_loop(lo, hi, lambda r, c: (zero_copy(r).wait(), c)[1], 0)
            return carry

        lax.fori_loop(0, N_EXPERTS, per_expert, 0)


def _dest_table(dest, tm):
    n_tok = dest.shape[1]
    return dest[:TOP_K].T.reshape(n_tok // tm, 1, tm * TOP_K)


def _dispatch(fill, dest, h2, n_rows):
    n_tok = h2.shape[0] // SUBLANES
    tm = ROUTE_TILE
    return pl.pallas_call(
        _dispatch_kernel,
        out_shape=jax.ShapeDtypeStruct((n_rows * SUBLANES, LANES), h2.dtype),
        grid_spec=pltpu.PrefetchScalarGridSpec(
            num_scalar_prefetch=1,
            grid=(n_tok // tm,),
            in_specs=[pl.BlockSpec((1, 1, tm * TOP_K), lambda i, f: (i, 0, 0), memory_space=pltpu.SMEM),
                      pl.BlockSpec(memory_space=pl.ANY)],
            out_specs=pl.BlockSpec(memory_space=pl.ANY),
            scratch_shapes=[pltpu.VMEM((3, tm * SUBLANES, LANES), h2.dtype),
                            pltpu.VMEM((SUBLANES, LANES), h2.dtype),
                            pltpu.SemaphoreType.DMA((3,)), pltpu.SemaphoreType.DMA((2,)),
                            pltpu.SemaphoreType.DMA(())]),
        compiler_params=_cparams(("arbitrary",)),
        name="dispatch",
    )(fill, _dest_table(dest, tm), h2)


COMBINE_TILE = 256


def _combine_kernel(dest_ref, dest_next_ref, x1_ref, gates_ref, gate2_ref, gf_ref, ys_ref, o_ref, buf_ref, sem):
    tm = x1_ref.shape[1]
    step = pl.program_id(0)
    slot = lax.rem(step, 2)

    def row_copy(table_ref, sl, t, kk, src=None):
        src = table_ref[0, 0, t * TOP_K + kk] if src is None else src
        return pltpu.make_async_copy(_tile_row(ys_ref, src), _tile_row(buf_ref.at[sl, kk], t), sem.at[sl])

    def gather(table_ref, sl):
        def issue(g, carry):
            t0 = g * DMA_UNROLL
            src = [[table_ref[0, 0, (t0 + u) * TOP_K + kk] for kk in range(TOP_K)] for u in range(DMA_UNROLL)]
            for u in range(DMA_UNROLL):
                for kk in range(TOP_K):
                    row_copy(table_ref, sl, t0 + u, kk, src[u][kk]).start(priority=kk % 2)
            return carry

        lax.fori_loop(0, tm // DMA_UNROLL, issue, 0)

    @pl.when(step == 0)
    def _():
        gather(dest_ref, slot)

    @pl.when(step + 1 < pl.num_programs(0))
    def _():
        gather(dest_next_ref, 1 - slot)

    def drain(t, carry):
        for kk in range(TOP_K):
            row_copy(dest_ref, slot, t, kk).wait()
        return carry

    lax.fori_loop(0, tm, drain, 0, unroll=DMA_UNROLL)

    gates = gates_ref[...]
    moe = gates[:, 0:1] * _load_tile_rows(buf_ref, (slot, 0), tm)
    for kk in range(1, TOP_K):
        moe = moe + gates[:, kk:kk + 1] * _load_tile_rows(buf_ref, (slot, kk), tm)
    x2 = x1_ref[0] + gate2_ref[0] * moe
    o_ref[0] = _rms(x2) * gf_ref[...]


def _combine(dest, x1, gates_t, gate2, gf, ys):
    b, l, d = x1.shape
    tm = COMBINE_TILE
    per_b = l // tm
    n_tiles = b * per_b
    table = _dest_table(dest, tm)
    table_spec = lambda nxt: pl.BlockSpec((1, 1, tm * TOP_K), lambda g: (jnp.minimum(g + nxt, n_tiles - 1), 0, 0),
                                          memory_space=pltpu.SMEM)
    return pl.pallas_call(
        _combine_kernel,
        out_shape=jax.ShapeDtypeStruct((b, l, d), F32),
        grid=(n_tiles,),
        in_specs=[table_spec(0), table_spec(1),
                  pl.BlockSpec((1, tm, d), lambda g: (g // per_b, g % per_b, 0)),
                  pl.BlockSpec((tm, TOP_K), lambda g: (g, 0)),
                  pl.BlockSpec((1, 1, d), lambda g: (g // per_b, 0, 0)),
                  _resident((1, d)),
                  pl.BlockSpec(memory_space=pl.ANY)],
        out_specs=pl.BlockSpec((1, tm, d), lambda g: (g // per_b, g % per_b, 0)),
        scratch_shapes=[pltpu.VMEM((2, TOP_K) + _tile_rows_shape(tm, d), ys.dtype),
                        pltpu.SemaphoreType.DMA((2,))],
        compiler_params=_cparams(("arbitrary",)),
        name="combine",
    )(table, table, x1, gates_t, gate2, gf, ys)


def kernel(x, c, ctx, c_ctx, norm1_g, norm2_g, ada_w, ada_b, w_in, ret_decay_fwd, ret_decay_bwd, mla_q_norm_g, mla_w_uq, mla_kv_norm_g, mla_w_ukv, w_branch_ret, w_branch_mla, w_out, router_w, router_b, exp_w_gu, exp_b_gu, exp_w_down, exp_b_down, final_norm_g):
    depth = norm1_g.shape[0]
    assert depth == 1, "single-layer block"
    b, l, d = x.shape
    qk_scale = float((MLA_QK_NOPE + MLA_QK_ROPE) ** -0.5 * np.log2(np.e))

    n_mod = b + 1
    rows = -(-n_mod // 8) * 8
    cvec = jnp.concatenate([c, c_ctx[None, :], jnp.zeros((rows - n_mod, d), F32)], axis=0)
    mod = _ada_mod(cvec, ada_w[0], ada_b[0])
    m_lat = [mod[:b, i * d:(i + 1) * d].reshape(b, 1, d) for i in range(6)]
    m_ctx = [mod[b:b + 1, i * d:(i + 1) * d] for i in range(2)]

    w_lat, w_ctx, uq, uk, uvt = _prep_weights(w_in[0], mla_w_uq[0], mla_w_ukv[0])
    tret, tmla = _rope_tables(l, qk_scale)
    g1 = norm1_g[0].reshape(1, d)
    gq = mla_q_norm_g[0].reshape(1, MLA_Q_RANK)
    gkv = mla_kv_norm_g[0].reshape(1, MLA_KV_RANK)

    rq, rk, rv, sgate, q, k_lat, vt_lat, sgr, sgm = _inproj_lat(
        x, m_lat[0], m_lat[1], g1, w_lat, uq, uk, uvt, gq, gkv, tret, tmla, qk_scale)
    rk_ctx, rv_ctx, k_ctx, vt_ctx = _inproj_ctx(ctx, m_ctx[0], m_ctx[1], g1, w_ctx, uk, uvt, gkv)

    lg = jnp.stack([jax.nn.log_sigmoid(ret_decay_fwd[0].astype(F32)),
                    jax.nn.log_sigmoid(ret_decay_bwd[0].astype(F32))])
    ret_o = _retention(lg, rq, rk, rv, rk_ctx, rv_ctx)
    att_o = _attention(q, k_lat, k_ctx, vt_lat, vt_ctx)

    wrt = jnp.pad(router_w[0], ((0, 0), (0, LANES - N_EXPERTS))).astype(BF16)
    brt = jnp.pad(router_b[0], (0, LANES - N_EXPERTS)).reshape(1, LANES)
    x1, h2, logits = _merge(x, ret_o, sgate, att_o, sgr, sgm, m_lat[2], m_lat[3], m_lat[4],
                            norm2_g[0].reshape(1, d), w_branch_ret[0].astype(BF16),
                            w_branch_mla[0].astype(BF16), w_out[0].astype(BF16), wrt, brt)

    n_tok = b * l
    idx, gates, rank, counts = _router(logits.reshape(n_tok, LANES))
    dest, block_e, n_used, fill, n_rows = _layout(idx, rank, counts, n_tok)
    xs = _dispatch(fill, dest, h2.reshape(_tile_rows_shape(n_tok, d)), n_rows)
    ys = _experts(block_e, n_used, xs, exp_w_gu[0], exp_b_gu[0], exp_w_down[0], exp_b_down[0])
    return _combine(dest, x1, gates[:TOP_K].T, m_lat[5], final_norm_g.reshape(1, d), ys)
```

```python
import functools

import numpy as np
import jax
import jax.numpy as jnp
from jax import lax
from jax.experimental import pallas as pl
from jax.experimental.pallas import tpu as pltpu

GRID_W = 64
N_RET_HEADS = 4
RET_QK_DIM = 256
RET_V_DIM = 512
N_MLA_HEADS = 8
MLA_Q_RANK = 384
MLA_KV_RANK = 256
MLA_QK_NOPE = 128
MLA_QK_ROPE = 64
MLA_V_DIM = 128
N_EXPERTS = 32
TOP_K = 4
SWIGLU_LIMIT = 7.0
SWIGLU_ALPHA = 1.702
ROPE_BASE = 10000.0
EPS = 1e-6

LANES = 128
MLA_HEAD_PAD = 2 * LANES
RET_CHUNK = 256
VMEM_LIMIT = 56 * 1024 * 1024
INPROJ_TILE = 256
MERGE_TILE = 512
MERGE_PARTS = 2

F32 = jnp.float32
BF16 = jnp.bfloat16


def _cparams(sem):
    return pltpu.CompilerParams(dimension_semantics=sem, vmem_limit_bytes=VMEM_LIMIT)


def _resident(shape):
    nd = len(shape)
    return pl.BlockSpec(shape, lambda *_: (0,) * nd, pipeline_mode=pl.Buffered(1))


def _dot(a, b):
    return jnp.dot(a, b, preferred_element_type=F32)


def _dot_nt(a, b):
    return lax.dot_general(a, b, (((1,), (1,)), ((), ())), preferred_element_type=F32)


def _dot_tn(a, b):
    return lax.dot_general(a, b, (((0,), (0,)), ((), ())), preferred_element_type=F32)


SUBLANES = 8


def _tile_rows_shape(n, d):
    assert d == SUBLANES * LANES
    return (n * SUBLANES, LANES)


def _store_tile_rows(ref, lead, val):
    rows = val.shape[0]
    for s in range(SUBLANES):
        ref[lead + (pl.ds(s, rows, stride=SUBLANES), slice(None))] = val[:, s * LANES:(s + 1) * LANES]


def _load_tile_rows(ref, lead, rows):
    return jnp.concatenate([ref[lead + (pl.ds(s, rows, stride=SUBLANES), slice(None))]
                            for s in range(SUBLANES)], axis=1)


def _tile_row(ref, r):
    return ref.at[pl.ds(pl.multiple_of(r * SUBLANES, SUBLANES), SUBLANES), :]


def _rms(x):
    return x * lax.rsqrt(jnp.mean(x * x, axis=-1, keepdims=True) + EPS)


def _sigmoid(x):
    return 1.0 / (1.0 + jnp.exp(-x))


def _ada_kernel(c_ref, w_ref, b_ref, o_ref):
    c = c_ref[...]
    s = c * _sigmoid(c)
    o_ref[...] = jnp.dot(s, w_ref[...], preferred_element_type=F32,
                         precision=lax.Precision.HIGHEST) + b_ref[...]


def _ada_mod(cvec, w, b):
    rows, d = cvec.shape
    n = w.shape[1]
    tn = 1024
    return pl.pallas_call(
        _ada_kernel,
        out_shape=jax.ShapeDtypeStruct((rows, n), F32),
        grid=(n // tn,),
        in_specs=[pl.BlockSpec((rows, d), lambda j: (0, 0)),
                  pl.BlockSpec((d, tn), lambda j: (0, j)),
                  pl.BlockSpec((1, tn), lambda j: (0, j))],
        out_specs=pl.BlockSpec((rows, tn), lambda j: (0, j)),
        compiler_params=_cparams(("arbitrary",)),
        name="ada_mod",
    )(cvec, w, b.reshape(1, n))


def _rope_tables(seq_len, qk_scale):
    pos = np.arange(seq_len)
    rows = (pos // GRID_W).astype(np.float32)
    cols = (pos % GRID_W).astype(np.float32)

    def angles(p, half):
        freqs = (np.float32(ROPE_BASE) ** (-np.arange(half, dtype=np.float32) / np.float32(half))).astype(np.float32)
        return (p[:, None] * freqs[None, :]).astype(np.float32)

    a_r, a_c = angles(rows, 64), angles(cols, 64)
    ret = np.concatenate([np.cos(a_r), np.cos(a_r), np.cos(a_c), np.cos(a_c),
                          -np.sin(a_r), np.sin(a_r), -np.sin(a_c), np.sin(a_c)], axis=1)
    b_r, b_c = angles(rows, 16), angles(cols, 16)
    z32 = np.zeros((seq_len, 32), np.float32)
    z64 = np.zeros((seq_len, 64), np.float32)
    cos = np.concatenate([np.cos(b_r), np.cos(b_c), np.cos(b_r), np.cos(b_c), z64], axis=1)
    sin_up = np.concatenate([-np.sin(b_r), -np.sin(b_c), z32, z64], axis=1)
    sin_dn = np.concatenate([z32, np.sin(b_r), np.sin(b_c), z64], axis=1)
    mla_k = np.concatenate([cos, sin_up, sin_dn], axis=1)
    mla = np.concatenate([mla_k, mla_k * np.float32(qk_scale)], axis=1)
    return jnp.asarray(ret, F32), jnp.asarray(mla, F32)


def _rope_ret(t, tab, parity):
    cos = tab[:, parity * LANES:(parity + 1) * LANES]
    sin = tab[:, (2 + parity) * LANES:(3 + parity) * LANES]
    return t * cos + pltpu.roll(t, 64, 1) * sin


def _rope_mla(t, tab, base):
    cos = tab[:, base:base + LANES]
    sin_up = tab[:, base + LANES:base + 2 * LANES]
    sin_dn = tab[:, base + 2 * LANES:base + 3 * LANES]
    return t * cos + pltpu.roll(t, 96, 1) * sin_up + pltpu.roll(t, 32, 1) * sin_dn


_C_Q, _C_K, _C_V, _C_G = 0, 1024, 2048, 4096
_C_CQ, _C_CKV, _C_KR, _C_GR, _C_GM, _C_END = 6144, 6528, 6784, 6912, 7936, 8960
_X_K, _X_V, _X_CKV, _X_KR, _X_END = 0, 1024, 3072, 3328, 3456


def _norm_mod(x, g, shift, scale):
    return (_rms(x) * g) * (1.0 + scale) + shift


def _mla_kv(ckv_acc, kr, gkv_ref, wuk_ref, wuvt_ref, k_ref, vt_ref):
    ckvn = (_rms(ckv_acc) * gkv_ref[...]).astype(BF16)
    kn = _dot(ckvn, wuk_ref[...])
    krb = kr.astype(BF16)
    for hh in range(N_MLA_HEADS):
        k_ref[0, :, hh * MLA_HEAD_PAD:hh * MLA_HEAD_PAD + LANES] = kn[:, hh * LANES:(hh + 1) * LANES].astype(BF16)
        k_ref[0, :, hh * MLA_HEAD_PAD + LANES:(hh + 1) * MLA_HEAD_PAD] = krb
    vt_ref[0] = _dot_nt(wuvt_ref[...], ckvn).astype(BF16)


def _inproj_lat_kernel(qk_scale, x_ref, shift_ref, scale_ref, g_ref, w_ref, wuq_ref, wuk_ref, wuvt_ref, gq_ref,
                       gkv_ref, tret_ref, tmla_ref,
                       rq_ref, rk_ref, rv_ref, sgate_ref, q_ref, k_ref, vt_ref, sgr_ref, sgm_ref):
    h = _norm_mod(x_ref[0], g_ref[...], shift_ref[0], scale_ref[0]).astype(BF16)
    tret = tret_ref[...]
    tmla = tmla_ref[...]

    def proj(c0, n):
        return _dot(h, w_ref[:, c0:c0 + n])

    cq_acc = proj(_C_CQ, MLA_Q_RANK)
    ckv_acc = proj(_C_CKV, MLA_KV_RANK)
    kr_acc = proj(_C_KR, LANES)

    for base, out in ((_C_Q, rq_ref), (_C_K, rk_ref)):
        for j in range(2):
            acc = proj(base + j * 512, 512)
            for blk in range(4):
                t = acc[:, blk * LANES:(blk + 1) * LANES]
                col = j * 512 + blk * LANES
                out[0, :, col:col + LANES] = _rope_ret(t, tret, blk % 2).astype(BF16)

    cqn = (_rms(cq_acc) * gq_ref[...]).astype(BF16)
    for j in range(4):
        acc = _dot(cqn, wuq_ref[:, j * 512:(j + 1) * 512])
        for blk in range(4):
            t = acc[:, blk * LANES:(blk + 1) * LANES]
            col = j * 512 + blk * LANES
            if blk % 2 == 0:
                q_ref[0, :, col:col + LANES] = (t * qk_scale).astype(BF16)
            else:
                q_ref[0, :, col:col + LANES] = _rope_mla(t, tmla, 3 * LANES).astype(BF16)
    _mla_kv(ckv_acc, _rope_mla(kr_acc, tmla, 0), gkv_ref, wuk_ref, wuvt_ref, k_ref, vt_ref)

    for j in range(4):
        a = proj(_C_G + j * 512, 512)
        sgate_ref[0, :, j * 512:(j + 1) * 512] = (a * _sigmoid(a)).astype(BF16)
    for j in range(2):
        a = proj(_C_GR + j * 512, 512)
        sgr_ref[0, :, j * 512:(j + 1) * 512] = _sigmoid(a).astype(BF16)
    for j in range(2):
        a = proj(_C_GM + j * 512, 512)
        sgm_ref[0, :, j * 512:(j + 1) * 512] = _sigmoid(a).astype(BF16)
    for j in range(4):
        rv_ref[0, :, j * 512:(j + 1) * 512] = proj(_C_V + j * 512, 512).astype(BF16)


def _inproj_ctx_kernel(x_ref, shift_ref, scale_ref, g_ref, w_ref, wuk_ref, wuvt_ref, gkv_ref,
                       rk_ref, rv_ref, k_ref, vt_ref):
    h = _norm_mod(x_ref[0], g_ref[...], shift_ref[...], scale_ref[...]).astype(BF16)

    def proj(c0, n):
        return _dot(h, w_ref[:, c0:c0 + n])

    for j in range(2):
        rk_ref[0, :, j * 512:(j + 1) * 512] = proj(_X_K + j * 512, 512).astype(BF16)
    for j in range(4):
        rv_ref[0, :, j * 512:(j + 1) * 512] = proj(_X_V + j * 512, 512).astype(BF16)
    _mla_kv(proj(_X_CKV, MLA_KV_RANK), proj(_X_KR, LANES), gkv_ref, wuk_ref, wuvt_ref, k_ref, vt_ref)


def _prep_weights(w_in, w_uq, w_ukv):
    nq = N_RET_HEADS * RET_QK_DIM
    nv = N_RET_HEADS * RET_V_DIM
    d = w_in.shape[0]
    sizes = (nq, nq, nv, nv, MLA_Q_RANK, MLA_KV_RANK, MLA_QK_ROPE, d, d)
    offs = np.cumsum((0,) + sizes)
    wq, wk, wv, wg, wcq, wckv, wkr, wgr, wgm = [w_in[:, offs[i]:offs[i + 1]] for i in range(9)]
    wk = wk * (RET_QK_DIM ** -0.5)
    perm = np.concatenate([np.arange(0, 16), np.arange(32, 48), np.arange(16, 32), np.arange(48, 64)])
    wkr = jnp.pad(wkr[:, perm], ((0, 0), (0, LANES - MLA_QK_ROPE)))
    w_lat = jnp.concatenate([wq, wk, wv, wg, wcq, wckv, wkr, wgr, wgm], axis=1).astype(BF16)
    w_ctx = jnp.concatenate([wk, wv, wckv, wkr], axis=1).astype(BF16)
    uq = w_uq.reshape(MLA_Q_RANK, N_MLA_HEADS, MLA_QK_NOPE + MLA_QK_ROPE)
    uq = jnp.concatenate([uq[:, :, :MLA_QK_NOPE], uq[:, :, MLA_QK_NOPE:][:, :, perm],
                          jnp.zeros((MLA_Q_RANK, N_MLA_HEADS, LANES - MLA_QK_ROPE), w_uq.dtype)], axis=2)
    uq = uq.reshape(MLA_Q_RANK, N_MLA_HEADS * MLA_HEAD_PAD).astype(BF16)
    ukv = w_ukv.reshape(MLA_KV_RANK, N_MLA_HEADS, MLA_QK_NOPE + MLA_V_DIM)
    uk = ukv[:, :, :MLA_QK_NOPE].reshape(MLA_KV_RANK, -1).astype(BF16)
    uvt = ukv[:, :, MLA_QK_NOPE:].reshape(MLA_KV_RANK, -1).T.astype(BF16)
    return w_lat, w_ctx, uq, uk, uvt


def _tok_spec(tm, n):
    return pl.BlockSpec((1, tm, n), lambda i, j: (i, j, 0))


def _vt_spec(tm):
    return pl.BlockSpec((1, N_MLA_HEADS * MLA_V_DIM, tm), lambda i, j: (i, 0, j))


def _inproj_lat(x, shift, scale, g, w_lat, uq, uk, uvt, gq, gkv, tret, tmla, qk_scale):
    b, l, d = x.shape
    tm = INPROJ_TILE
    vec = pl.BlockSpec((1, 1, d), lambda i, j: (i, 0, 0))
    out_w = (1024, 1024, 2048, 2048, 2048, 2048, None, 1024, 1024)
    nvt = N_MLA_HEADS * MLA_V_DIM
    return pl.pallas_call(
        functools.partial(_inproj_lat_kernel, qk_scale),
        out_shape=[jax.ShapeDtypeStruct((b, nvt, l) if n is None else (b, l, n), BF16) for n in out_w],
        grid=(b, l // tm),
        in_specs=[_tok_spec(tm, d), vec, vec, _resident((1, d)), _resident(w_lat.shape), _resident(uq.shape),
                  _resident(uk.shape), _resident(uvt.shape), _resident((1, MLA_Q_RANK)),
                  _resident((1, MLA_KV_RANK)),
                  pl.BlockSpec((tm, tret.shape[1]), lambda i, j: (j, 0)),
                  pl.BlockSpec((tm, tmla.shape[1]), lambda i, j: (j, 0))],
        out_specs=[_vt_spec(tm) if n is None else _tok_spec(tm, n) for n in out_w],
        compiler_params=_cparams(("arbitrary", "arbitrary")),
        name="inproj_lat",
    )(x, shift, scale, g, w_lat, uq, uk, uvt, gq, gkv, tret, tmla)


def _inproj_ctx(ctx, shift, scale, g, w_ctx, uk, uvt, gkv):
    b, l, d = ctx.shape
    tm = 256
    out_w = (1024, 2048, 2048, None)
    nvt = N_MLA_HEADS * MLA_V_DIM
    return pl.pallas_call(
        _inproj_ctx_kernel,
        out_shape=[jax.ShapeDtypeStruct((b, nvt, l) if n is None else (b, l, n), BF16) for n in out_w],
        grid=(b, l // tm),
        in_specs=[_tok_spec(tm, d), _resident((1, d)), _resident((1, d)), _resident((1, d)),
                  _resident(w_ctx.shape), _resident(uk.shape), _resident(uvt.shape),
                  _resident((1, MLA_KV_RANK))],
        out_specs=[_vt_spec(tm) if n is None else _tok_spec(tm, n) for n in out_w],
        compiler_params=_cparams(("arbitrary", "arbitrary")),
        name="inproj_ctx",
    )(ctx, shift, scale, g, w_ctx, uk, uvt, gkv)


def _retention_kernel(n_chunks, n_ctx_chunks, lg_ref, q_ref, k_ref, v_ref, kc_ref, vc_ref, o_ref,
                      ob_ref, sf_ref, sb_ref, dmat_ref, dec_ref):
    c = RET_CHUNK
    head = pl.program_id(1)
    lgf = lg_ref[0, head]
    lgb = lg_ref[1, head]
    ri = lax.broadcasted_iota(jnp.int32, (c, c), 0).astype(F32)
    ci = lax.broadcasted_iota(jnp.int32, (c, c), 1).astype(F32)
    diff = ri - ci
    dmat_ref[...] = jnp.where(diff >= 0, jnp.exp(lgf * jnp.maximum(diff, 0.0)),
                              jnp.exp(lgb * jnp.maximum(-diff, 0.0)))
    rk = lax.broadcasted_iota(jnp.int32, (c, RET_QK_DIM), 0).astype(F32)
    dec_ref[0] = jnp.exp(lgf * (rk + 1.0))
    dec_ref[1] = jnp.exp(lgf * (c - 1.0 - rk))
    dec_ref[2] = jnp.exp(lgb * (c - rk))
    dec_ref[3] = jnp.exp(lgb * rk)
    cdf = jnp.exp(jnp.full((1, RET_V_DIM), lgf * c, F32))
    cdb = jnp.exp(jnp.full((1, RET_V_DIM), lgb * c, F32))

    def scaled(t, which):
        return (t.astype(F32) * dec_ref[which]).astype(BF16)

    sf_ref[...] = jnp.zeros_like(sf_ref)
    sb_ref[...] = jnp.zeros_like(sb_ref)
    for n in range(n_ctx_chunks):
        sl = slice(n * c, (n + 1) * c)
        sf_ref[...] = sf_ref[...] * cdf + _dot_tn(scaled(kc_ref[0, sl, :], 1), vc_ref[0, sl, :])
    for n in reversed(range(n_ctx_chunks)):
        sl = slice(n * c, (n + 1) * c)
        sb_ref[...] = sb_ref[...] * cdb + _dot_tn(scaled(kc_ref[0, sl, :], 3), vc_ref[0, sl, :])

    def step(i, finish):
        rb = pl.multiple_of((n_chunks - 1 - i) * c, c)
        rf = pl.multiple_of(i * c, c)
        qf = q_ref[0, pl.ds(rf, c), :]
        kf = k_ref[0, pl.ds(rf, c), :]
        vf = v_ref[0, pl.ds(rf, c), :]
        qb = q_ref[0, pl.ds(rb, c), :]
        kb = k_ref[0, pl.ds(rb, c), :]
        vb = v_ref[0, pl.ds(rb, c), :]
        s = _dot_nt(qf, kf)
        o_b = _dot(scaled(qb, 2), sb_ref[...].astype(BF16))
        sb_new = _dot_tn(scaled(kb, 3), vb)
        o_f = _dot(scaled(qf, 0), sf_ref[...].astype(BF16))
        sf_new = _dot_tn(scaled(kf, 1), vf)
        o_f = o_f + _dot((s * dmat_ref[...]).astype(BF16), vf)
        sb_ref[...] = sb_ref[...] * cdb + sb_new
        sf_ref[...] = sf_ref[...] * cdf + sf_new
        if finish:
            o_ref[0, pl.ds(rb, c), :] = _rms(o_b + ob_ref[pl.ds(rb, c), :]).astype(BF16)
            o_ref[0, pl.ds(rf, c), :] = _rms(o_f + ob_ref[pl.ds(rf, c), :]).astype(BF16)
        else:
            ob_ref[pl.ds(rb, c), :] = o_b
            ob_ref[pl.ds(rf, c), :] = o_f

    half = n_chunks // 2
    lax.fori_loop(0, half, lambda i, carry: (step(i, False), carry)[1], 0)
    lax.fori_loop(half, n_chunks, lambda i, carry: (step(i, True), carry)[1], 0)


def _retention(lg, rq, rk, rv, rk_ctx, rv_ctx):
    b, l, _ = rq.shape
    lc = rk_ctx.shape[1]
    c = RET_CHUNK
    assert l % (2 * c) == 0 and lc % c == 0
    qk = lambda n: pl.BlockSpec((1, n, RET_QK_DIM), lambda i, h: (i, 0, h))
    vv = lambda n: pl.BlockSpec((1, n, RET_V_DIM), lambda i, h: (i, 0, h))
    return pl.pallas_call(
        functools.partial(_retention_kernel, l // c, lc // c),
        out_shape=jax.ShapeDtypeStruct((b, l, N_RET_HEADS * RET_V_DIM), BF16),
        grid=(b, N_RET_HEADS),
        in_specs=[pl.BlockSpec(memory_space=pltpu.SMEM), qk(l), qk(l), vv(l), qk(lc), vv(lc)],
        out_specs=vv(l),
        scratch_shapes=[pltpu.VMEM((l, RET_V_DIM), F32),
                        pltpu.VMEM((RET_QK_DIM, RET_V_DIM), F32),
                        pltpu.VMEM((RET_QK_DIM, RET_V_DIM), F32),
                        pltpu.VMEM((c, c), F32),
                        pltpu.VMEM((4, c, RET_QK_DIM), F32)],
        compiler_params=_cparams(("arbitrary", "arbitrary")),
        name="retention",
    )(lg, rq, rk, rv, rk_ctx, rv_ctx)


ATTN_CHAIN = 256
ATTN_CHAINS = 8


ATTN_KEY_BLOCK = 1024


def _attention_kernel(q_ref, kl_ref, kc_ref, vtl_ref, vtc_ref, o_ref):
    l = kl_ref.shape[1]
    lc = kc_ref.shape[1]
    blocks = [(kc_ref, vtc_ref, 0, lc)]
    blocks += [(kl_ref, vtl_ref, r0, ATTN_KEY_BLOCK) for r0 in range(0, l, ATTN_KEY_BLOCK)]
    chains = range(ATTN_CHAINS)
    rows = [slice(ch * ATTN_CHAIN, (ch + 1) * ATTN_CHAIN) for ch in chains]
    q = [q_ref[0, rows[ch], :] for ch in chains]

    def scores(ch, j):
        k_ref, _, r0, n = blocks[j]
        return _dot_nt(k_ref[0, r0:r0 + n, :], q[ch])

    s = [scores(ch, 0) for ch in chains]
    m = [None] * ATTN_CHAINS
    denom = [None] * ATTN_CHAINS
    acc = [None] * ATTN_CHAINS
    for j in range(len(blocks)):
        _, vt_ref, r0, n = blocks[j]
        for ch in chains:
            s_cur = s[ch]
            if j + 1 < len(blocks):
                s[ch] = scores(ch, j + 1)
            m_blk = jnp.max(s_cur, axis=0, keepdims=True)
            if j == 0:
                m[ch] = m_blk
                p = jnp.exp2(s_cur - m_blk)
                denom[ch] = jnp.sum(p, axis=0, keepdims=True)
                acc[ch] = _dot(vt_ref[0, :, r0:r0 + n], p.astype(BF16))
            else:
                m_new = jnp.maximum(m[ch], m_blk)
                alpha = jnp.exp2(m[ch] - m_new)
                p = jnp.exp2(s_cur - m_new)
                denom[ch] = denom[ch] * alpha + jnp.sum(p, axis=0, keepdims=True)
                acc[ch] = acc[ch] * alpha + _dot(vt_ref[0, :, r0:r0 + n], p.astype(BF16))
                m[ch] = m_new
    for ch in chains:
        o_ref[0, rows[ch], :] = (acc[ch] / denom[ch]).T.astype(BF16)


def _attention(q, k_lat, k_ctx, vt_lat, vt_ctx):
    b, l, _ = q.shape
    lc = k_ctx.shape[1]
    tq = ATTN_CHAIN * ATTN_CHAINS
    return pl.pallas_call(
        _attention_kernel,
        out_shape=jax.ShapeDtypeStruct((b, l, N_MLA_HEADS * MLA_V_DIM), BF16),
        grid=(b, N_MLA_HEADS, l // tq),
        in_specs=[pl.BlockSpec((1, tq, MLA_HEAD_PAD), lambda i, h, j: (i, j, h)),
                  pl.BlockSpec((1, l, MLA_HEAD_PAD), lambda i, h, j: (i, 0, h)),
                  pl.BlockSpec((1, lc, MLA_HEAD_PAD), lambda i, h, j: (i, 0, h)),
                  pl.BlockSpec((1, MLA_V_DIM, l), lambda i, h, j: (i, h, 0)),
                  pl.BlockSpec((1, MLA_V_DIM, lc), lambda i, h, j: (i, h, 0))],
        out_specs=pl.BlockSpec((1, tq, MLA_V_DIM), lambda i, h, j: (i, j, h)),
        compiler_params=_cparams(("arbitrary", "arbitrary", "arbitrary")),
        name="attention",
    )(q, k_lat, k_ctx, vt_lat, vt_ctx)


def _merge_kernel(x_ref, ret_ref, sgate_ref, att_ref, sgr_ref, sgm_ref, gate1_ref, shift2_ref, scale2_ref,
                  g2_ref, wr_ref, wm_ref, wo_ref, wrt_ref, brt_ref,
                  x1_ref, h2_ref, idx_ref, gate_ref, rank_ref, cnt_ref, base_ref, tri_ref):
    half = x_ref.shape[1] // MERGE_PARTS
    parts = [slice(p * half, (p + 1) * half) for p in range(MERGE_PARTS)]
    branch = []
    for rows in parts:
        r = (ret_ref[0, rows, :].astype(F32) * sgate_ref[0, rows, :].astype(F32)).astype(BF16)
        branch.append((_dot(r, wr_ref[...]), _dot(att_ref[0, rows, :], wm_ref[...])))
    ys = []
    for rows, (a_ret, a_mla) in zip(parts, branch):
        merged = sgr_ref[0, rows, :].astype(F32) * a_ret + sgm_ref[0, rows, :].astype(F32) * a_mla
        ys.append(_dot(merged.astype(BF16), wo_ref[...]))
    logits = []
    for rows, y in zip(parts, ys):
        x1 = x_ref[0, rows, :] + gate1_ref[0] * y
        x1_ref[0, rows, :] = x1
        h2 = _norm_mod(x1, g2_ref[...], shift2_ref[0], scale2_ref[0])
        for s in range(SUBLANES):
            h2_ref[0, pl.ds(rows.start * SUBLANES + s, half, stride=SUBLANES), :] = h2[:, s * LANES:(s + 1) * LANES]
        logits.append(_dot(h2.astype(BF16), wrt_ref[...]) + brt_ref[...])
    first = (pl.program_id(0) == 0) & (pl.program_id(1) == 0)
    _route_tile(jnp.concatenate(logits, axis=0), first, idx_ref, gate_ref, rank_ref, cnt_ref, base_ref, tri_ref)


def _merge(x, ret_o, sgate, att_o, sgr, sgm, gate1, shift2, scale2, g2, wr, wm, wo, wrt, brt):
    b, l, d = x.shape
    tm = MERGE_TILE
    tok = lambda n: pl.BlockSpec((1, tm, n), lambda i, j: (i, j, 0))
    vec = pl.BlockSpec((1, 1, d), lambda i, j: (i, 0, 0))
    per_b = l // tm
    n_tok = b * l
    tab = lambda: pl.BlockSpec((ROUTE_ROWS, tm), lambda i, j: (0, i * per_b + j))
    return pl.pallas_call(
        _merge_kernel,
        out_shape=[jax.ShapeDtypeStruct((b, l, d), F32),
                   jax.ShapeDtypeStruct((b,) + _tile_rows_shape(l, d), F32),
                   jax.ShapeDtypeStruct((ROUTE_ROWS, n_tok), jnp.int32),
                   jax.ShapeDtypeStruct((ROUTE_ROWS, n_tok), F32),
                   jax.ShapeDtypeStruct((ROUTE_ROWS, n_tok), jnp.int32),
                   jax.ShapeDtypeStruct((N_EXPERTS, LANES), F32)],
        grid=(b, per_b),
        in_specs=[tok(d), tok(ret_o.shape[2]), tok(sgate.shape[2]), tok(att_o.shape[2]), tok(d), tok(d),
                  vec, vec, vec, _resident((1, d)), _resident(wr.shape), _resident(wm.shape),
                  _resident(wo.shape), _resident(wrt.shape), _resident(brt.shape)],
        out_specs=[tok(d), pl.BlockSpec((1,) + _tile_rows_shape(tm, d), lambda i, j: (i, j, 0)),
                   tab(), tab(), tab(), pl.BlockSpec((N_EXPERTS, LANES), lambda i, j: (0, 0))],
        scratch_shapes=[pltpu.VMEM((N_EXPERTS, LANES), F32), pltpu.VMEM((tm, tm), BF16)],
        compiler_params=_cparams(("arbitrary", "arbitrary")),
        name="merge",
    )(x, ret_o, sgate, att_o, sgr, sgm, gate1, shift2, scale2, g2, wr, wm, wo, wrt, brt)


EXPERT_BLOCK = 512
EXPERT_PARTS = 1


def _expert_kernel(be_ref, nb_ref, x_ref, wgu_ref, bgu_ref, wd_ref, bd_ref, y_ref, wgu_bf, wd_bf):
    step = pl.program_id(0)

    @pl.when(step >= nb_ref[0])
    def _():
        y_ref[...] = jnp.zeros_like(y_ref)

    @pl.when((step < nb_ref[0]) & ((step == 0) | (be_ref[step] != be_ref[jnp.maximum(step - 1, 0)])))
    def _():
        cw = 4 * LANES
        for c0 in range(0, wgu_bf.shape[1], cw):
            wgu_bf[:, c0:c0 + cw] = wgu_ref[0, :, c0:c0 + cw].astype(BF16)
        for c0 in range(0, wd_bf.shape[1], cw):
            wd_bf[:, c0:c0 + cw] = wd_ref[0, :, c0:c0 + cw].astype(BF16)

    @pl.when(step < nb_ref[0])
    def _():
        f = wd_bf.shape[0]
        rows = EXPERT_BLOCK // EXPERT_PARTS
        gu = []
        for p in range(EXPERT_PARTS):
            x = jnp.concatenate([x_ref[pl.ds(p * rows * SUBLANES + s, rows, stride=SUBLANES), :]
                                 for s in range(SUBLANES)], axis=1).astype(BF16)
            gu.append((_dot(x, wgu_bf[:, :f]) + bgu_ref[0, :, :f], _dot(x, wgu_bf[:, f:]) + bgu_ref[0, :, f:]))
        for p, (gate, up) in enumerate(gu):
            gate = jnp.minimum(gate, SWIGLU_LIMIT)
            up = jnp.clip(up, -SWIGLU_LIMIT, SWIGLU_LIMIT)
            glu = gate * _sigmoid(SWIGLU_ALPHA * gate)
            act = ((up + 1.0) * glu).astype(BF16)
            y = _dot(act, wd_bf[...]) + bd_ref[0]
            for s in range(SUBLANES):
                y_ref[pl.ds(p * rows * SUBLANES + s, rows, stride=SUBLANES), :] = y[:, s * LANES:(s + 1) * LANES]


def _experts(block_e, n_used, xs, w_gu, b_gu, w_down, b_down):
    n_rows = xs.shape[0] // SUBLANES
    e, d, f2 = w_gu.shape
    f = f2 // 2
    bm = EXPERT_BLOCK
    blk = _tile_rows_shape(bm, d)
    return pl.pallas_call(
        _expert_kernel,
        out_shape=jax.ShapeDtypeStruct(xs.shape, F32),
        grid_spec=pltpu.PrefetchScalarGridSpec(
            num_scalar_prefetch=2,
            grid=(n_rows // bm,),
            in_specs=[pl.BlockSpec(blk, lambda i, be, nb: (jnp.minimum(i, nb[0] - 1), 0)),
                      pl.BlockSpec((1, d, f2), lambda i, be, nb: (be[i], 0, 0)),
                      pl.BlockSpec((1, 1, f2), lambda i, be, nb: (be[i], 0, 0)),
                      pl.BlockSpec((1, f, d), lambda i, be, nb: (be[i], 0, 0)),
                      pl.BlockSpec((1, 1, d), lambda i, be, nb: (be[i], 0, 0))],
            out_specs=pl.BlockSpec(blk, lambda i, be, nb: (i, 0)),
            scratch_shapes=[pltpu.VMEM((d, f2), BF16), pltpu.VMEM((f, d), BF16)]),
        compiler_params=_cparams(("arbitrary",)),
        name="experts",
    )(block_e, n_used, xs, w_gu, b_gu.reshape(e, 1, f2), w_down, b_down.reshape(e, 1, d))


ROUTE_TILE = 512
ROUTE_ROWS = 8
DMA_UNROLL = 8


def _route_tile(logits, first, idx_ref, gate_ref, rank_ref, cnt_ref, base_ref, tri_ref):
    tm = logits.shape[0]

    @pl.when(first)
    def _():
        base_ref[...] = jnp.zeros_like(base_ref)
        r = lax.broadcasted_iota(jnp.int32, (tm, tm), 0)
        c = lax.broadcasted_iota(jnp.int32, (tm, tm), 1)
        tri_ref[...] = jnp.where(r < c, 1.0, 0.0).astype(BF16)

    v = logits.T[:N_EXPERTS, :]
    eid = lax.broadcasted_iota(jnp.int32, (N_EXPERTS, tm), 0)
    onehot = jnp.zeros((N_EXPERTS, tm), F32)
    vals, sels = [], []
    for kk in range(TOP_K):
        mx = jnp.max(v, axis=0, keepdims=True)
        ik = jnp.min(jnp.where(v == mx, eid, N_EXPERTS), axis=0, keepdims=True)
        sel = eid == ik
        idx_ref[kk:kk + 1, :] = ik
        vals.append(mx)
        sels.append(sel)
        onehot = onehot + jnp.where(sel, 1.0, 0.0)
        v = jnp.where(sel, -jnp.inf, v)
    ex = [jnp.exp(val - vals[0]) for val in vals]
    tot = ex[0] + ex[1] + ex[2] + ex[3]
    for kk in range(TOP_K):
        gate_ref[kk:kk + 1, :] = ex[kk] / tot
    rank_e = base_ref[:, 0:1] + _dot(onehot.astype(BF16), tri_ref[...])
    for kk in range(TOP_K):
        rank_ref[kk:kk + 1, :] = jnp.sum(jnp.where(sels[kk], rank_e, 0.0), axis=0, keepdims=True).astype(jnp.int32)
    zero_i = jnp.zeros((ROUTE_ROWS - TOP_K, tm), jnp.int32)
    idx_ref[TOP_K:, :] = zero_i
    rank_ref[TOP_K:, :] = zero_i
    gate_ref[TOP_K:, :] = jnp.zeros((ROUTE_ROWS - TOP_K, tm), F32)
    base_ref[...] = base_ref[...] + jnp.sum(onehot, axis=1, keepdims=True)
    cnt_ref[...] = base_ref[...]


def _layout(idx, rank, counts, n_tok):
    bm = EXPERT_BLOCK
    counts = counts[:, 0].astype(jnp.int32)
    padded = (counts + bm - 1) // bm * bm
    pad_end = jnp.cumsum(padded)
    pad_start = pad_end - padded
    n_rows = n_tok * TOP_K + N_EXPERTS * bm
    n_blocks = n_rows // bm
    blk_start = jnp.arange(n_blocks, dtype=jnp.int32) * bm
    block_e = jnp.minimum(jnp.sum(blk_start[:, None] >= pad_end[None, :], axis=1), N_EXPERTS - 1).astype(jnp.int32)
    n_used = (pad_end[-1] // bm).astype(jnp.int32).reshape(1)
    start_of = jnp.zeros_like(idx)
    for e in range(N_EXPERTS):
        start_of = jnp.where(idx == e, pad_start[e], start_of)
    dest = start_of + rank
    fill = jnp.stack([pad_start + counts, pad_end]).astype(jnp.int32)
    return dest, block_e, n_used, fill, n_rows


def _dispatch_kernel(fill_ref, dest_ref, h_ref, xs_ref, hbuf_ref, zero_ref, lsem, sem, zsem):
    tm = dest_ref.shape[2] // TOP_K
    step = pl.program_id(0)
    last = pl.num_programs(0) - 1
    par = lax.rem(step, 2)
    slot = lax.rem(step, 3)

    def tile_copy(i, sl):
        rows = tm * SUBLANES
        return pltpu.make_async_copy(h_ref.at[pl.ds(pl.multiple_of(i * rows, rows), rows), :],
                                     hbuf_ref.at[sl], lsem.at[sl])

    @pl.when(step == 0)
    def _():
        tile_copy(0, 0).start()

    @pl.when(step < last)
    def _():
        tile_copy(step + 1, lax.rem(step + 1, 3)).start()

    tile_copy(step, slot).wait()
    src_ref = hbuf_ref.at[slot]

    def row_copy(t, kk, sl, dst=None):
        dst = dest_ref[0, 0, t * TOP_K + kk] if dst is None else dst
        return pltpu.make_async_copy(_tile_row(src_ref, t), _tile_row(xs_ref, dst), sem.at[sl])

    def issue(g, carry):
        t0 = g * DMA_UNROLL
        dst = [[dest_ref[0, 0, (t0 + u) * TOP_K + kk] for kk in range(TOP_K)] for u in range(DMA_UNROLL)]
        for u in range(DMA_UNROLL):
            for kk in range(TOP_K):
                row_copy(t0 + u, kk, par, dst[u][kk]).start(priority=kk % 2)
        return carry

    lax.fori_loop(0, tm // DMA_UNROLL, issue, 0)

    def drain(sl):
        def body(t, carry):
            for kk in range(TOP_K):
                row_copy(t, kk, sl).wait()
            return carry

        lax.fori_loop(0, tm, body, 0, unroll=DMA_UNROLL)

    @pl.when(step > 0)
    def _():
        drain(1 - par)

    @pl.when(step == last)
    def _():
        drain(par)

    @pl.when(step == last)
    def _():
        zero_ref[...] = jnp.zeros_like(zero_ref)

        def zero_copy(r):
            return pltpu.make_async_copy(zero_ref, _tile_row(xs_ref, r), zsem)

        def per_expert(e, carry):
            lo = fill_ref[0, e]
            hi = fill_ref[1, e]
            lax.fori_loop(lo, hi, lambda r, c: (zero_copy(r).start(), c)[1], 0)
            lax.fori_loop(lo, hi, lambda r, c: (zero_copy(r).wait(), c)[1], 0)
            return carry

        lax.fori_loop(0, N_EXPERTS, per_expert, 0)


def _dest_table(dest, tm):
    n_tok = dest.shape[1]
    return dest[:TOP_K].T.reshape(n_tok // tm, 1, tm * TOP_K)


def _dispatch(fill, dest, h2, n_rows):
    n_tok = h2.shape[0] // SUBLANES
    tm = ROUTE_TILE
    return pl.pallas_call(
        _dispatch_kernel,
        out_shape=jax.ShapeDtypeStruct((n_rows * SUBLANES, LANES), h2.dtype),
        grid_spec=pltpu.PrefetchScalarGridSpec(
            num_scalar_prefetch=1,
            grid=(n_tok // tm,),
            in_specs=[pl.BlockSpec((1, 1, tm * TOP_K), lambda i, f: (i, 0, 0), memory_space=pltpu.SMEM),
                      pl.BlockSpec(memory_space=pl.ANY)],
            out_specs=pl.BlockSpec(memory_space=pl.ANY),
            scratch_shapes=[pltpu.VMEM((3, tm * SUBLANES, LANES), h2.dtype),
                            pltpu.VMEM((SUBLANES, LANES), h2.dtype),
                            pltpu.SemaphoreType.DMA((3,)), pltpu.SemaphoreType.DMA((2,)),
                            pltpu.SemaphoreType.DMA(())]),
        compiler_params=_cparams(("arbitrary",)),
        name="dispatch",
    )(fill, _dest_table(dest, tm), h2)


COMBINE_TILE = 512


def _combine_kernel(dest_ref, dest_next_ref, x1_ref, gates_ref, gate2_ref, gf_ref, ys_ref, o_ref, buf_ref, sem):
    tm = x1_ref.shape[1]
    step = pl.program_id(0)
    slot = lax.rem(step, 2)

    def row_copy(table_ref, sl, t, kk, src=None):
        src = table_ref[0, 0, t * TOP_K + kk] if src is None else src
        return pltpu.make_async_copy(_tile_row(ys_ref, src), _tile_row(buf_ref.at[sl, kk], t), sem.at[sl])

    def gather(table_ref, sl):
        def issue(g, carry):
            t0 = g * DMA_UNROLL
            src = [[table_ref[0, 0, (t0 + u) * TOP_K + kk] for kk in range(TOP_K)] for u in range(DMA_UNROLL)]
            for u in range(DMA_UNROLL):
                for kk in range(TOP_K):
                    row_copy(table_ref, sl, t0 + u, kk, src[u][kk]).start(priority=kk % 2)
            return carry

        lax.fori_loop(0, tm // DMA_UNROLL, issue, 0)

    @pl.when(step == 0)
    def _():
        gather(dest_ref, slot)

    @pl.when(step + 1 < pl.num_programs(0))
    def _():
        gather(dest_next_ref, 1 - slot)

    def drain(t, carry):
        for kk in range(TOP_K):
            row_copy(dest_ref, slot, t, kk).wait()
        return carry

    lax.fori_loop(0, tm, drain, 0, unroll=DMA_UNROLL)

    gates = gates_ref[...]
    moe = gates[:, 0:1] * _load_tile_rows(buf_ref, (slot, 0), tm)
    for kk in range(1, TOP_K):
        moe = moe + gates[:, kk:kk + 1] * _load_tile_rows(buf_ref, (slot, kk), tm)
    x2 = x1_ref[0] + gate2_ref[0] * moe
    o_ref[0] = _rms(x2) * gf_ref[...]


def _combine(dest, x1, gates_t, gate2, gf, ys):
    b, l, d = x1.shape
    tm = COMBINE_TILE
    per_b = l // tm
    n_tiles = b * per_b
    table = _dest_table(dest, tm)
    table_spec = lambda nxt: pl.BlockSpec((1, 1, tm * TOP_K), lambda g: (jnp.minimum(g + nxt, n_tiles - 1), 0, 0),
                                          memory_space=pltpu.SMEM)
    return pl.pallas_call(
        _combine_kernel,
        out_shape=jax.ShapeDtypeStruct((b, l, d), F32),
        grid=(n_tiles,),
        in_specs=[table_spec(0), table_spec(1),
                  pl.BlockSpec((1, tm, d), lambda g: (g // per_b, g % per_b, 0)),
                  pl.BlockSpec((tm, TOP_K), lambda g: (g, 0)),
                  pl.BlockSpec((1, 1, d), lambda g: (g // per_b, 0, 0)),
                  _resident((1, d)),
                  pl.BlockSpec(memory_space=pl.ANY)],
        out_specs=pl.BlockSpec((1, tm, d), lambda g: (g // per_b, g % per_b, 0)),
        scratch_shapes=[pltpu.VMEM((2, TOP_K) + _tile_rows_shape(tm, d), ys.dtype),
                        pltpu.SemaphoreType.DMA((2,))],
        compiler_params=_cparams(("arbitrary",)),
        name="combine",
    )(table, table, x1, gates_t, gate2, gf, ys)


def kernel(x, c, ctx, c_ctx, norm1_g, norm2_g, ada_w, ada_b, w_in, ret_decay_fwd, ret_decay_bwd, mla_q_norm_g, mla_w_uq, mla_kv_norm_g, mla_w_ukv, w_branch_ret, w_branch_mla, w_out, router_w, router_b, exp_w_gu, exp_b_gu, exp_w_down, exp_b_down, final_norm_g):
    depth = norm1_g.shape[0]
    assert depth == 1, "single-layer block"
    b, l, d = x.shape
    qk_scale = float((MLA_QK_NOPE + MLA_QK_ROPE) ** -0.5 * np.log2(np.e))

    n_mod = b + 1
    rows = -(-n_mod // 8) * 8
    cvec = jnp.concatenate([c, c_ctx[None, :], jnp.zeros((rows - n_mod, d), F32)], axis=0)
    mod = _ada_mod(cvec, ada_w[0], ada_b[0])
    m_lat = [mod[:b, i * d:(i + 1) * d].reshape(b, 1, d) for i in range(6)]
    m_ctx = [mod[b:b + 1, i * d:(i + 1) * d] for i in range(2)]

    w_lat, w_ctx, uq, uk, uvt = _prep_weights(w_in[0], mla_w_uq[0], mla_w_ukv[0])
    tret, tmla = _rope_tables(l, qk_scale)
    g1 = norm1_g[0].reshape(1, d)
    gq = mla_q_norm_g[0].reshape(1, MLA_Q_RANK)
    gkv = mla_kv_norm_g[0].reshape(1, MLA_KV_RANK)

    rq, rk, rv, sgate, q, k_lat, vt_lat, sgr, sgm = _inproj_lat(
        x, m_lat[0], m_lat[1], g1, w_lat, uq, uk, uvt, gq, gkv, tret, tmla, qk_scale)
    rk_ctx, rv_ctx, k_ctx, vt_ctx = _inproj_ctx(ctx, m_ctx[0], m_ctx[1], g1, w_ctx, uk, uvt, gkv)

    lg = jnp.stack([jax.nn.log_sigmoid(ret_decay_fwd[0].astype(F32)),
                    jax.nn.log_sigmoid(ret_decay_bwd[0].astype(F32))])
    ret_o = _retention(lg, rq, rk, rv, rk_ctx, rv_ctx)
    att_o = _attention(q, k_lat, k_ctx, vt_lat, vt_ctx)

    wrt = jnp.pad(router_w[0], ((0, 0), (0, LANES - N_EXPERTS))).astype(BF16)
    brt = jnp.pad(router_b[0], (0, LANES - N_EXPERTS)).reshape(1, LANES)
    x1, h2, idx, gates, rank, counts = _merge(
        x, ret_o, sgate, att_o, sgr, sgm, m_lat[2], m_lat[3], m_lat[4], norm2_g[0].reshape(1, d),
        w_branch_ret[0].astype(BF16), w_branch_mla[0].astype(BF16), w_out[0].astype(BF16), wrt, brt)

    n_tok = b * l
    dest, block_e, n_used, fill, n_rows = _layout(idx, rank, counts, n_tok)
    xs = _dispatch(fill, dest, h2.reshape(_tile_rows_shape(n_tok, d)), n_rows)
    ys = _experts(block_e, n_used, xs, exp_w_gu[0], exp_b_gu[0], exp_w_down[0], exp_b_down[0])
    return _combine(dest, x1, gates[:TOP_K].T, m_lat[5], final_norm_g.reshape(1, d), ys)
```
